```python
import jax, jax.numpy as jnp
from jax import lax
import numpy as np

D_MODEL = 1024
BATCH = 2
SEQ = 8192
DEPTH = 2
DEC_BATCH = 32
DEC_SEQ = 4
PAST_LEN = 8192
PAGE_SIZE = 128

N_SELF = DEPTH // 2
MLSTM_HEADS = 8
MLSTM_DK = D_MODEL // 16
MLSTM_DV = D_MODEL // 8
MLSTM_CHUNK = 128
SB_HEADS = 16
SB_HEAD_DIM = D_MODEL // SB_HEADS
SB_BLOCK = 128
SB_BIAS_INIT = -6.0
D_FF = 4 * D_MODEL
EPS = 1e-6

kernel_name = 'yoco_mlstm_stickbreak_step'


def rmsnorm(x, g):
    xf = x.astype(jnp.float32)
    y = xf * lax.rsqrt(jnp.mean(xf * xf, axis=-1, keepdims=True) + EPS)
    return (y * g.astype(jnp.float32)).astype(x.dtype)


def mlstm_chunkwise(q, k, v, ig, lf, c0, n0, m0):
    B, T, H, DK = q.shape
    DV = v.shape[-1]
    L = MLSTM_CHUNK if T % MLSTM_CHUNK == 0 else T
    nc = T // L

    def to_chunks(a):
        a = a.astype(jnp.float32).reshape((B, nc, L, H) + a.shape[3:])
        return jnp.moveaxis(a, (1, 3), (0, 2))

    xs = (to_chunks(q), to_chunks(k), to_chunks(v), to_chunks(ig), to_chunks(lf))
    causal = jnp.tril(jnp.ones((L, L), dtype=bool))

    def step(carry, inp):
        c, n, m = carry
        qi, ki, vi, igi, lfi = inp
        b = jnp.cumsum(lfi, axis=-1)
        log_d = jnp.where(causal, b[..., :, None] - b[..., None, :] + igi[..., None, :], -jnp.inf)
        m_inter = b + m[..., None]
        m_t = jnp.maximum(m_inter, jnp.max(log_d, axis=-1))
        w_inter = jnp.exp(m_inter - m_t)
        s = jnp.einsum('bhtd,bhsd->bhts', qi, ki) * jnp.exp(log_d - m_t[..., None])
        num = w_inter[..., None] * jnp.einsum('bhtd,bhde->bhte', qi, c) + jnp.einsum('bhts,bhse->bhte', s, vi)
        den = w_inter * jnp.einsum('bhtd,bhd->bht', qi, n) + jnp.sum(s, axis=-1)
        h = num / jnp.maximum(jnp.abs(den), jnp.exp(-m_t))[..., None]
        m_new = m_t[..., -1]
        w_k = jnp.exp(b[..., -1:] - b + igi - m_new[..., None])
        decay = jnp.exp(b[..., -1] + m - m_new)
        c_new = decay[..., None, None] * c + jnp.einsum('bhs,bhsd,bhse->bhde', w_k, ki, vi)
        n_new = decay[..., None] * n + jnp.einsum('bhs,bhsd->bhd', w_k, ki)
        return (c_new, n_new, m_new), h

    init = (c0.astype(jnp.float32), n0.astype(jnp.float32), m0.astype(jnp.float32))
    (c, n, m), h = lax.scan(step, init, xs)
    h = jnp.moveaxis(h, (0, 2), (1, 3)).reshape(B, T, H, DV)
    return h, (c, n, m)


def mlstm_layer(x, c0, n0, m0, w_in, b_i, b_f, g_head, w_out):
    B, T, _ = x.shape
    qk = MLSTM_HEADS * MLSTM_DK
    vd = MLSTM_HEADS * MLSTM_DV
    proj = x @ w_in
    q, k, v, o, gi, gf = jnp.split(proj, [qk, 2 * qk, 2 * qk + vd, 2 * qk + 2 * vd, 2 * qk + 2 * vd + MLSTM_HEADS], axis=-1)
    q = q.reshape(B, T, MLSTM_HEADS, MLSTM_DK)
    k = k.reshape(B, T, MLSTM_HEADS, MLSTM_DK) * (MLSTM_DK ** -0.5)
    v = v.reshape(B, T, MLSTM_HEADS, MLSTM_DV)
    ig = gi.astype(jnp.float32) + b_i.astype(jnp.float32)
    lf = jax.nn.log_sigmoid(gf.astype(jnp.float32) + b_f.astype(jnp.float32))
    h, state = mlstm_chunkwise(q, k, v, ig, lf, c0, n0, m0)
    h = rmsnorm(h.astype(x.dtype), g_head).reshape(B, T, vd)
    h = jax.nn.sigmoid(o) * h
    return h @ w_out, state


def stick_breaking_block(q, k, v, bias, q_pos, k_pos):
    z = jnp.einsum('bqhd,bkhd->bhqk', q, k).astype(jnp.float32) * (SB_HEAD_DIM ** -0.5)
    z = z + bias.astype(jnp.float32)[None, :, None, None]
    mask = k_pos[None, :] < q_pos[:, None]
    log1mb = jnp.where(mask, jax.nn.log_sigmoid(-z), 0.0)
    log_w = jnp.where(mask, z + lax.cumsum(log1mb, axis=3, reverse=True), -jnp.inf)
    a = jnp.exp(log_w)
    return jnp.einsum('bhqk,bkhd->bqhd', a.astype(v.dtype), v)


def stick_breaking_attention(q, k, v, bias, past_len):
    T = q.shape[1]
    blk = SB_BLOCK if T % SB_BLOCK == 0 else T
    nb = T // blk
    k_pos = jnp.arange(k.shape[1])

    def one_block(i):
        qb = lax.dynamic_slice_in_dim(q, i * blk, blk, axis=1)
        q_pos = past_len + i * blk + jnp.arange(blk)
        return stick_breaking_block(qb, k, v, bias, q_pos, k_pos)

    out = lax.map(one_block, jnp.arange(nb))
    return jnp.moveaxis(out, 0, 1).reshape(q.shape)


def trunk(x, past_len, past_k, past_v, state_c, state_n, state_m,
          norm_mix, norm_mlp, w_in_a, b_i, b_f, head_norm_a, w_out_a,
          norm_kv, w_kv, w_q_b, w_out_b, sb_bias, w_up, w_down, norm_final):
    B, T, D = x.shape
    h = x
    new_c, new_n, new_m = [], [], []
    k_new = v_new = k_all = v_all = None
    for l in range(DEPTH):
        if l < N_SELF:
            mix, (c, n, m) = mlstm_layer(rmsnorm(h, norm_mix[l]), state_c[l], state_n[l], state_m[l],
                                         w_in_a[l], b_i[l], b_f[l], head_norm_a[l], w_out_a[l])
            new_c.append(c)
            new_n.append(n)
            new_m.append(m)
        else:
            if l == N_SELF:
                kv = rmsnorm(h, norm_kv) @ w_kv
                k_new, v_new = jnp.split(kv.reshape(B, T, 2, SB_HEADS, SB_HEAD_DIM), 2, axis=2)
                k_new, v_new = k_new[:, :, 0], v_new[:, :, 0]
                k_all = jnp.concatenate([past_k.astype(k_new.dtype), k_new], axis=1)
                v_all = jnp.concatenate([past_v.astype(v_new.dtype), v_new], axis=1)
            j = l - N_SELF
            q = (rmsnorm(h, norm_mix[l]) @ w_q_b[j]).reshape(B, T, SB_HEADS, SB_HEAD_DIM)
            o = stick_breaking_attention(q, k_all, v_all, sb_bias[j], past_len)
            mix = o.reshape(B, T, SB_HEADS * SB_HEAD_DIM) @ w_out_b[j]
        h = h + mix.astype(h.dtype)
        u = rmsnorm(h, norm_mlp[l]) @ w_up[l]
        h = h + jnp.square(jax.nn.relu(u)) @ w_down[l]
    y = rmsnorm(h, norm_final)
    return y, jnp.stack(new_c), jnp.stack(new_n), jnp.stack(new_m), k_new, v_new


def setup_inputs(seed: int = 0) -> dict:
    key = jax.random.key(seed)
    ks = jax.random.split(key, 24)
    f32 = jnp.float32
    n_pages = PAST_LEN // PAGE_SIZE
    n_used = DEC_BATCH * n_pages
    n_phys = n_used + (n_used + 3) // 4
    n_cross = DEPTH - N_SELF
    in_a = 2 * MLSTM_HEADS * MLSTM_DK + 2 * MLSTM_HEADS * MLSTM_DV + 2 * MLSTM_HEADS
    sbw = SB_HEADS * SB_HEAD_DIM

    def nrm(k, shape, scale=1.0):
        return jax.random.normal(k, shape, f32) * scale

    def gain(k, shape):
        return 1.0 + 0.02 * jax.random.normal(k, shape, f32)

    page_table = jax.random.permutation(ks[7], n_phys)[:n_used].reshape(DEC_BATCH, n_pages).astype(jnp.int32)
    return {
        'x_prompt': nrm(ks[0], (BATCH, SEQ, D_MODEL)),
        'x_sample': nrm(ks[1], (DEC_BATCH, DEC_SEQ, D_MODEL)),
        'state_c': nrm(ks[2], (N_SELF, DEC_BATCH, MLSTM_HEADS, MLSTM_DK, MLSTM_DV), 0.1),
        'state_n': nrm(ks[3], (N_SELF, DEC_BATCH, MLSTM_HEADS, MLSTM_DK), 0.1),
        'state_m': nrm(ks[4], (N_SELF, DEC_BATCH, MLSTM_HEADS)),
        'cache_k': nrm(ks[5], (n_phys, PAGE_SIZE, SB_HEADS, SB_HEAD_DIM)),
        'cache_v': nrm(ks[6], (n_phys, PAGE_SIZE, SB_HEADS, SB_HEAD_DIM)),
        'page_table': page_table,
        'norm_mix': gain(ks[8], (DEPTH, D_MODEL)),
        'norm_mlp': gain(ks[9], (DEPTH, D_MODEL)),
        'w_in_a': nrm(ks[10], (N_SELF, D_MODEL, in_a), D_MODEL ** -0.5),
        'b_i': nrm(ks[11], (N_SELF, MLSTM_HEADS), 0.1),
        'b_f': 3.0 + 3.0 * jax.random.uniform(ks[12], (N_SELF, MLSTM_HEADS), f32),
        'head_norm_a': gain(ks[13], (N_SELF, MLSTM_HEADS, MLSTM_DV)),
        'w_out_a': nrm(ks[14], (N_SELF, MLSTM_HEADS * MLSTM_DV, D_MODEL), (MLSTM_HEADS * MLSTM_DV) ** -0.5),
        'norm_kv': gain(ks[15], (D_MODEL,)),
        'w_kv': nrm(ks[16], (D_MODEL, 2 * sbw), D_MODEL ** -0.5),
        'w_q_b': nrm(ks[17], (n_cross, D_MODEL, sbw), D_MODEL ** -0.5),
        'w_out_b': nrm(ks[18], (n_cross, sbw, D_MODEL), sbw ** -0.5),
        'sb_bias': SB_BIAS_INIT + 0.5 * jax.random.normal(ks[22], (n_cross, SB_HEADS), f32),
        'w_up': nrm(ks[19], (DEPTH, D_MODEL, D_FF), D_MODEL ** -0.5),
        'w_down': nrm(ks[20], (DEPTH, D_FF, D_MODEL), D_FF ** -0.5),
        'norm_final': gain(ks[21], (D_MODEL,)),
    }


def reference(x_prompt, x_sample, state_c, state_n, state_m, cache_k, cache_v, page_table,
              norm_mix, norm_mlp, w_in_a, b_i, b_f, head_norm_a, w_out_a,
              norm_kv, w_kv, w_q_b, w_out_b, sb_bias, w_up, w_down, norm_final):
    bp = x_prompt.shape[0]
    db, n_pages = page_table.shape
    past_len = n_pages * cache_k.shape[1]
    c0 = jnp.zeros((N_SELF, bp, MLSTM_HEADS, MLSTM_DK, MLSTM_DV), jnp.float32)
    n0 = jnp.zeros((N_SELF, bp, MLSTM_HEADS, MLSTM_DK), jnp.float32)
    m0 = jnp.zeros((N_SELF, bp, MLSTM_HEADS), jnp.float32)
    empty = jnp.zeros((bp, 0, SB_HEADS, SB_HEAD_DIM), x_prompt.dtype)
    y_prompt, c_p, n_p, m_p, k_p, v_p = trunk(
        x_prompt, 0, empty, empty, c0, n0, m0,
        norm_mix, norm_mlp, w_in_a, b_i, b_f, head_norm_a, w_out_a,
        norm_kv, w_kv, w_q_b, w_out_b, sb_bias, w_up, w_down, norm_final)
    past_k = cache_k[page_table].reshape(db, past_len, SB_HEADS, SB_HEAD_DIM)
    past_v = cache_v[page_table].reshape(db, past_len, SB_HEADS, SB_HEAD_DIM)
    y_sample, c_s, n_s, m_s, k_s, v_s = trunk(
        x_sample, past_len, past_k, past_v, state_c, state_n, state_m,
        norm_mix, norm_mlp, w_in_a, b_i, b_f, head_norm_a, w_out_a,
        norm_kv, w_kv, w_q_b, w_out_b, sb_bias, w_up, w_down, norm_final)
    return (y_prompt, y_sample, c_p, n_p, m_p, k_p, v_p, c_s, n_s, m_s, k_s, v_s)
```

```python
import functools

import jax
import jax.numpy as jnp
from jax import lax
from jax.experimental import pallas as pl
from jax.experimental.pallas import tpu as pltpu

F32 = jnp.float32
BF16 = jnp.bfloat16

EPS = 1e-6
MLSTM_HEADS = 8
MLSTM_DK = 64
MLSTM_DV = 128
MLSTM_CHUNK = 128
SB_HEADS = 16
SB_HEAD_DIM = 64

V7X_LANES = 128
V7X_VMEM_BYTES = 64 * 1024 * 1024
VMEM_LIMIT_BYTES = V7X_VMEM_BYTES - 8 * 1024 * 1024

HEADS_PER_LANE_TILE = V7X_LANES // SB_HEAD_DIM
NEG_INF = float("-inf")


def _compiler_params(semantics):
    return pltpu.CompilerParams(dimension_semantics=semantics, vmem_limit_bytes=VMEM_LIMIT_BYTES)


def _dot(a, b):
    return jnp.dot(a, b, preferred_element_type=F32)


def _dot_nt(a, b):
    return lax.dot_general(a, b, (((1,), (1,)), ((), ())), preferred_element_type=F32)


def _dot_exact(a, b):
    return jnp.dot(a, b, preferred_element_type=F32, precision=lax.Precision.HIGHEST)


def _rms_scale(x):
    return lax.rsqrt(jnp.mean(x * x, axis=-1, keepdims=True) + EPS)


def _softplus(z):
    return jnp.maximum(z, 0.0) + jnp.log1p(jnp.exp(-jnp.abs(z)))


def _row_tile(t, target):
    tile = min(t, target)
    assert t % tile == 0, (t, tile)
    return tile


def _inproj_kernel(x_ref, g_ref, wq_ref, wkt_ref, wv_ref, wo_ref, wg_ref,
                   q_ref, kt_ref, v_ref, o_ref, gate_ref):
    x = x_ref[...]
    xb = (x * _rms_scale(x) * g_ref[...]).astype(BF16)
    q_ref[...] = _dot(xb, wq_ref[...]).astype(BF16)
    kt_ref[...] = _dot_nt(wkt_ref[...], xb).astype(BF16)
    v_ref[...] = _dot(xb, wv_ref[...]).astype(BF16)
    o_ref[...] = _dot(xb, wo_ref[...])
    gate_ref[...] = _dot(xb, wg_ref[...])


def _inproj(x, gain, wq, wkt, wv, wo, wg):
    t, d = x.shape
    tm = _row_tile(t, 512)
    qk, vd, gw = wq.shape[1], wv.shape[1], wg.shape[1]
    full = lambda i: (0, 0)
    rows = lambda i: (i, 0)
    return pl.pallas_call(
        _inproj_kernel,
        grid=(t // tm,),
        in_specs=[
            pl.BlockSpec((tm, d), rows),
            pl.BlockSpec((1, d), full),
            pl.BlockSpec((d, qk), full),
            pl.BlockSpec((qk, d), full),
            pl.BlockSpec((d, vd), full),
            pl.BlockSpec((d, vd), full),
            pl.BlockSpec((d, gw), full),
        ],
        out_specs=[
            pl.BlockSpec((tm, qk), rows),
            pl.BlockSpec((qk, tm), lambda i: (0, i)),
            pl.BlockSpec((tm, vd), rows),
            pl.BlockSpec((tm, vd), rows),
            pl.BlockSpec((tm, gw), rows),
        ],
        out_shape=[
            jax.ShapeDtypeStruct((t, qk), BF16),
            jax.ShapeDtypeStruct((qk, t), BF16),
            jax.ShapeDtypeStruct((t, vd), BF16),
            jax.ShapeDtypeStruct((t, vd), F32),
            jax.ShapeDtypeStruct((t, gw), F32),
        ],
        compiler_params=_compiler_params(("parallel",)),
        name="mlstm_inproj",
    )(x, gain, wq, wkt, wv, wo, wg)


def _mlstm_kernel(q_ref, kt_ref, v_ref, o_ref, g_ref, bias_ref, ghead_ref, c0_ref, m0_ref,
                  hg_ref, c_ref, m_ref, rows_scr, *, valid_len):
    n_heads, dk, dv, chunk = MLSTM_HEADS, MLSTM_DK, MLSTM_DV, MLSTM_CHUNK

    @pl.when(pl.program_id(1) == 0)
    def _load_state():
        c_ref[...] = c0_ref[...]
        m_ref[...] = m0_ref[...]

    lane = lax.broadcasted_iota(jnp.int32, (chunk, V7X_LANES), 1)
    sub = lax.broadcasted_iota(jnp.int32, (chunk, V7X_LANES), 0)

    gates = g_ref[...] + bias_ref[...]
    log_f = jnp.minimum(gates, 0.0) - jnp.log1p(jnp.exp(-jnp.abs(gates)))
    if valid_len < chunk:
        valid = sub < valid_len
        log_f = jnp.where(valid, log_f, 0.0)
        gates = jnp.where(valid, gates, NEG_INF)
    is_f_lane = (lane >= n_heads) & (lane < 2 * n_heads)
    col_form = jnp.where(is_f_lane, log_f, jnp.where(lane < n_heads, gates, 0.0))
    row_form = col_form.T

    lower = (lane <= sub).astype(F32)
    upper = (sub <= lane).astype(F32)
    b_cols = _dot_exact(lower, jnp.where(is_f_lane, col_form, 0.0))
    is_f_row = (sub >= n_heads) & (sub < 2 * n_heads)
    b_rows = _dot_exact(jnp.where(is_f_row, row_form, 0.0), upper)
    rows_scr[0:chunk, :] = row_form
    rows_scr[chunk:2 * chunk, :] = b_rows

    causal = lane <= sub
    ones_col = jnp.where(lane == 0, 1.0, 0.0).astype(BF16)

    for h in range(n_heads):
        pair, half = divmod(h, HEADS_PER_LANE_TILE)
        b_col = jnp.sum(jnp.where(lane == n_heads + h, b_cols, 0.0), axis=1, keepdims=True)
        ig_row = rows_scr[h:h + 1, :]
        b_row = rows_scr[chunk + n_heads + h:chunk + n_heads + h + 1, :]
        m_prev = m_ref[0, h:h + 1, :][:, 0:1]

        log_d = jnp.where(causal, b_col - b_row + ig_row, NEG_INF)
        m_inter = b_col + m_prev
        m_t = jnp.maximum(m_inter, jnp.max(log_d, axis=1, keepdims=True))
        w_inter = jnp.exp(m_inter - m_t)
        decay_mat = jnp.exp(log_d - m_t)

        q_pair = q_ref[:, pair * V7X_LANES:(pair + 1) * V7X_LANES].astype(F32)
        q_h = jnp.where(lane // dk == half, q_pair, 0.0).astype(BF16)
        kt_pair = kt_ref[pair * V7X_LANES:(pair + 1) * V7X_LANES, :]
        s = _dot(q_h, kt_pair) * decay_mat

        v_h = v_ref[:, h * dv:(h + 1) * dv]
        c_h = c_ref[0, h]
        c_pair = jnp.concatenate([c_h, c_h], axis=0).astype(BF16)
        q_c = _dot(q_h, c_pair)
        q_n = jnp.sum(jnp.where(lane == 0, q_c[:, dv:], 0.0), axis=1, keepdims=True)
        num = w_inter * q_c[:, :dv] + _dot(s.astype(BF16), v_h)
        den = w_inter * q_n + jnp.sum(s, axis=1, keepdims=True)
        h_val = num / jnp.maximum(jnp.abs(den), jnp.exp(-m_t))

        y = h_val * _rms_scale(h_val) * ghead_ref[:, h * dv:(h + 1) * dv]
        gate_o = jax.nn.sigmoid(o_ref[:, h * dv:(h + 1) * dv])
        hg_ref[:, h * dv:(h + 1) * dv] = (gate_o * y).astype(BF16)

        m_new = m_t[chunk - 1:chunk, :]
        b_last = b_col[chunk - 1:chunk, :]
        w_k = jnp.exp(b_last - b_row + ig_row - m_new)
        decay = jnp.exp(b_last + m_prev - m_new)
        kt_h = kt_pair[half * dk:(half + 1) * dk, :].astype(F32)
        v_aug = jnp.concatenate([v_h, ones_col], axis=1)
        c_ref[0, h] = decay * c_h + _dot((kt_h * w_k).astype(BF16), v_aug)
        m_ref[0, h:h + 1, :] = jnp.broadcast_to(m_new, (1, V7X_LANES))


def _mlstm(q, kt, v, o, gates, gate_bias, ghead, c_aug0, m0, *, batch, valid_len):
    t_total = q.shape[0]
    chunk = MLSTM_CHUNK
    n_chunks = t_total // (batch * chunk)
    qk, vd, gw = q.shape[1], v.shape[1], gates.shape[1]
    rows = lambda b, c: (b * n_chunks + c, 0)
    full = lambda b, c: (0, 0)
    state4 = lambda b, c: (b, 0, 0, 0)
    state3 = lambda b, c: (b, 0, 0)
    return pl.pallas_call(
        functools.partial(_mlstm_kernel, valid_len=valid_len),
        grid=(batch, n_chunks),
        in_specs=[
            pl.BlockSpec((chunk, qk), rows),
            pl.BlockSpec((qk, chunk), lambda b, c: (0, b * n_chunks + c)),
            pl.BlockSpec((chunk, vd), rows),
            pl.BlockSpec((chunk, vd), rows),
            pl.BlockSpec((chunk, gw), rows),
            pl.BlockSpec((1, gw), full),
            pl.BlockSpec((1, vd), full),
            pl.BlockSpec((1,) + c_aug0.shape[1:], state4),
            pl.BlockSpec((1,) + m0.shape[1:], state3),
        ],
        out_specs=[
            pl.BlockSpec((chunk, vd), rows),
            pl.BlockSpec((1,) + c_aug0.shape[1:], state4),
            pl.BlockSpec((1,) + m0.shape[1:], state3),
        ],
        out_shape=[
            jax.ShapeDtypeStruct((t_total, vd), BF16),
            jax.ShapeDtypeStruct(c_aug0.shape, F32),
            jax.ShapeDtypeStruct(m0.shape, F32),
        ],
        scratch_shapes=[pltpu.VMEM((2 * chunk, V7X_LANES), F32)],
        compiler_params=_compiler_params(("parallel", "arbitrary")),
        name="mlstm_chunk",
    )(q, kt, v, o, gates, gate_bias, ghead, c_aug0, m0)


def _proj_residual_kernel(a_ref, w_ref, res_ref, out_ref):
    out_ref[...] = res_ref[...] + _dot(a_ref[...], w_ref[...])


def _proj_residual(a, w, res):
    t, k = a.shape
    n = w.shape[1]
    tm = _row_tile(t, 512)
    rows = lambda i: (i, 0)
    return pl.pallas_call(
        _proj_residual_kernel,
        grid=(t // tm,),
        in_specs=[pl.BlockSpec((tm, k), rows), pl.BlockSpec((k, n), lambda i: (0, 0)),
                  pl.BlockSpec((tm, n), rows)],
        out_specs=pl.BlockSpec((tm, n), rows),
        out_shape=jax.ShapeDtypeStruct((t, n), F32),
        compiler_params=_compiler_params(("parallel",)),
        name="proj_residual",
    )(a, w, res)


def _mlp_kernel(x_ref, g_ref, wup_ref, wdn_ref, gfin_ref, out_ref, xn_scr, acc_scr, *, final_norm):
    j = pl.program_id(1)

    @pl.when(j == 0)
    def _start():
        x = x_ref[...]
        xn_scr[...] = (x * _rms_scale(x) * g_ref[...]).astype(BF16)
        acc_scr[...] = x

    u = _dot(xn_scr[...], wup_ref[...])
    act = jnp.square(jnp.maximum(u, 0.0)).astype(BF16)
    acc_scr[...] += _dot(act, wdn_ref[...])

    @pl.when(j == pl.num_programs(1) - 1)
    def _finish():
        y = acc_scr[...]
        if final_norm:
            y = y * _rms_scale(y) * gfin_ref[...]
        out_ref[...] = y


def _mlp(x, gain, w_up, w_down, gain_final, *, final_norm):
    t, d = x.shape
    ff = w_up.shape[1]
    tm = _row_tile(t, 1024)
    tf = _row_tile(ff, 1024)
    return pl.pallas_call(
        functools.partial(_mlp_kernel, final_norm=final_norm),
        grid=(t // tm, ff // tf),
        in_specs=[
            pl.BlockSpec((tm, d), lambda i, j: (i, 0)),
            pl.BlockSpec((1, d), lambda i, j: (0, 0)),
            pl.BlockSpec((d, tf), lambda i, j: (0, j)),
            pl.BlockSpec((tf, d), lambda i, j: (j, 0)),
            pl.BlockSpec((1, d), lambda i, j: (0, 0)),
        ],
        out_specs=pl.BlockSpec((tm, d), lambda i, j: (i, 0)),
        out_shape=jax.ShapeDtypeStruct((t, d), F32),
        scratch_shapes=[pltpu.VMEM((tm, d), BF16), pltpu.VMEM((tm, d), F32)],
        compiler_params=_compiler_params(("parallel", "arbitrary")),
        name="mlp",
    )(x, gain, w_up, w_down, gain_final)


def _kvq_kernel(h_ref, gkv_ref, gq_ref, wkv_ref, wq_ref, k_ref, v_ref, kb_ref, vb_ref, qb_ref):
    h = h_ref[...]
    hn = h * _rms_scale(h)
    kv = _dot((hn * gkv_ref[...]).astype(BF16), wkv_ref[...])
    d = k_ref.shape[1]
    k, v = kv[:, :d], kv[:, d:]
    k_ref[...] = k
    v_ref[...] = v
    kb_ref[...] = k.astype(BF16)
    vb_ref[...] = v.astype(BF16)
    qb_ref[...] = _dot((hn * gq_ref[...]).astype(BF16), wq_ref[...]).astype(BF16)


def _kvq(h, gain_kv, gain_q, w_kv, w_q):
    t, d = h.shape
    tm = _row_tile(t, 512)
    rows = lambda i: (i, 0)
    full = lambda i: (0, 0)
    n = w_q.shape[1]
    return pl.pallas_call(
        _kvq_kernel,
        grid=(t // tm,),
        in_specs=[pl.BlockSpec((tm, d), rows), pl.BlockSpec((1, d), full), pl.BlockSpec((1, d), full),
                  pl.BlockSpec((d, 2 * n), full), pl.BlockSpec((d, n), full)],
        out_specs=[pl.BlockSpec((tm, n), rows)] * 5,
        out_shape=[jax.ShapeDtypeStruct((t, n), F32), jax.ShapeDtypeStruct((t, n), F32),
                   jax.ShapeDtypeStruct((t, n), BF16), jax.ShapeDtypeStruct((t, n), BF16),
                   jax.ShapeDtypeStruct((t, n), BF16)],
        compiler_params=_compiler_params(("parallel",)),
        name="kvq_proj",
    )(h, gain_kv, gain_q, w_kv, w_q)


def _suffix_matrix(tk):
    j = lax.broadcasted_iota(jnp.int32, (tk, tk + V7X_LANES), 0)
    s = lax.broadcasted_iota(jnp.int32, (tk, tk + V7X_LANES), 1)
    return ((j >= s) | (s >= tk)).astype(BF16)


def _sb_prompt_kernel(bias_ref, q_ref, k_ref, v_ref, suffix_ref, out_ref, *, tq, tk):
    pair = pl.program_id(1)
    i = pl.program_id(2)
    n_rep = tk // V7X_LANES
    lane = lax.broadcasted_iota(jnp.int32, (tq, V7X_LANES), 1)
    q_pair = q_ref[...].astype(F32)
    q_heads = [jnp.where(lane // SB_HEAD_DIM == e, q_pair, 0.0).astype(BF16)
               for e in range(HEADS_PER_LANE_TILE)]
    biases = [bias_ref[pair * HEADS_PER_LANE_TILE + e] for e in range(HEADS_PER_LANE_TILE)]
    suffix = suffix_ref[...]

    def block(j, carry, masked):
        start = pl.multiple_of(j * tk, tk)
        k2 = k_ref[pl.ds(start, tk), :]
        v2 = v_ref[pl.ds(start, tk), :]
        if masked:
            row = lax.broadcasted_iota(jnp.int32, (tq, tk), 0)
            col = lax.broadcasted_iota(jnp.int32, (tq, tk), 1)
            visible = col < row
        new = []
        for e in range(HEADS_PER_LANE_TILE):
            total, acc = carry[e]
            z = _dot_nt(q_heads[e], k2) + biases[e]
            log1mb = -_softplus(z)
            if masked:
                log1mb = jnp.where(visible, log1mb, 0.0)
            sums = _dot(log1mb.astype(BF16), suffix)
            log_w = z + sums[:, :tk] + jnp.concatenate([total] * n_rep, axis=1)
            a = jnp.exp(log_w)
            if masked:
                a = jnp.where(visible, a, 0.0)
            acc = acc + _dot(a.astype(BF16), v2)
            new.append((total + sums[:, tk:], acc))
        return tuple(new)

    zero = jnp.zeros((tq, V7X_LANES), F32)
    carry = tuple((zero, zero) for _ in range(HEADS_PER_LANE_TILE))
    carry = block(i, carry, True)
    carry = lax.fori_loop(0, i, lambda it, c: block(i - 1 - it, c, False), carry)
    out = carry[0][1]
    for e in range(1, HEADS_PER_LANE_TILE):
        out = jnp.where(lane // SB_HEAD_DIM == e, carry[e][1], out)
    out_ref[...] = out.astype(BF16)


def _sb_prompt(q, k, v, bias, *, batch):
    t_total, d = q.shape
    t = t_total // batch
    tq = tk = _row_tile(t, 256)
    nq = t // tq
    n_pairs = d // V7X_LANES
    return pl.pallas_call(
        functools.partial(_sb_prompt_kernel, tq=tq, tk=tk),
        grid=(batch, n_pairs, nq),
        in_specs=[
            pl.BlockSpec(memory_space=pltpu.SMEM),
            pl.BlockSpec((tq, V7X_LANES), lambda b, p, i: (b * nq + i, p)),
            pl.BlockSpec((t, V7X_LANES), lambda b, p, i: (b, p)),
            pl.BlockSpec((t, V7X_LANES), lambda b, p, i: (b, p)),
            pl.BlockSpec((tk, tk + V7X_LANES), lambda b, p, i: (0, 0)),
        ],
        out_specs=pl.BlockSpec((tq, V7X_LANES), lambda b, p, i: (b * nq + i, p)),
        out_shape=jax.ShapeDtypeStruct((t_total, d), BF16),
        compiler_params=_compiler_params(("parallel", "parallel", "arbitrary")),
        name="sb_attention_prompt",
    )(bias, q, k, v, _suffix_matrix(tk))


def _sb_paged_kernel(pt_ref, q_ref, bias_ref, knew_ref, vnew_ref, kpage_ref, vpage_ref, out_ref,
                     qbd_scr, total_scr, acc_scr, *, n_q, page):
    del pt_ref
    step = pl.program_id(1)
    n_rows = n_q * SB_HEADS
    d = q_ref.shape[2]
    n_rep = page // V7X_LANES
    row_d = lax.broadcasted_iota(jnp.int32, (SB_HEADS, d), 0)
    col_d = lax.broadcasted_iota(jnp.int32, (SB_HEADS, d), 1)
    head_cols = col_d // SB_HEAD_DIM == row_d

    j = lax.broadcasted_iota(jnp.int32, (page, page + V7X_LANES), 0)
    s = lax.broadcasted_iota(jnp.int32, (page, page + V7X_LANES), 1)
    suffix = ((j >= s) | (s >= page)).astype(BF16)

    def visit(kt_page, vt_page, visible):
        z = _dot(qbd_scr[...], kt_page) + bias_ref[...]
        log1mb = -_softplus(z)
        if visible is not None:
            log1mb = jnp.where(visible, log1mb, 0.0)
        sums = _dot(log1mb.astype(BF16), suffix)
        total = total_scr[...]
        a = jnp.exp(z + sums[:, :page] + jnp.concatenate([total] * n_rep, axis=1))
        if visible is not None:
            a = jnp.where(visible, a, 0.0)
        acc_scr[...] += _dot_nt(a.astype(BF16), vt_page)
        total_scr[...] = total + sums[:, page:]

    @pl.when(step == 0)
    def _new_tokens():
        q = q_ref[0]
        blocks = [jnp.where(head_cols, jnp.broadcast_to(q[i:i + 1, :], (SB_HEADS, d)), 0.0)
                  for i in range(n_q)]
        qbd_scr[...] = jnp.concatenate(blocks, axis=0).astype(BF16)
        total_scr[...] = jnp.zeros_like(total_scr)
        acc_scr[...] = jnp.zeros_like(acc_scr)
        q_idx = lax.broadcasted_iota(jnp.int32, (n_rows, page), 0) // SB_HEADS
        key_idx = lax.broadcasted_iota(jnp.int32, (n_rows, page), 1)
        visit(knew_ref[0], vnew_ref[0], key_idx < q_idx)

    @pl.when(step > 0)
    def _past_page():
        visit(kpage_ref[0].astype(BF16), vpage_ref[0].astype(BF16), None)

    @pl.when(step == pl.num_programs(1) - 1)
    def _emit():
        acc = acc_scr[...]
        rows = []
        for i in range(n_q):
            blk = jnp.where(head_cols, acc[i * SB_HEADS:(i + 1) * SB_HEADS, :], 0.0)
            rows.append(jnp.sum(blk, axis=0, keepdims=True))
        out_ref[0] = jnp.concatenate(rows, axis=0).astype(BF16)


def _sb_paged(q, bias_rows, kt_new_pad, vt_new_pad, cache_kt, cache_vt, page_table):
    batch, n_q, d = q.shape
    n_pages = page_table.shape[1]
    page = cache_kt.shape[2]
    n_rows = n_q * SB_HEADS

    def page_index(b, s, pt):
        logical = n_pages - jnp.maximum(s, 1)
        return (pt[b, logical], 0, 0)

    grid_spec = pltpu.PrefetchScalarGridSpec(
        num_scalar_prefetch=1,
        grid=(batch, n_pages + 1),
        in_specs=[
            pl.BlockSpec((1, n_q, d), lambda b, s, pt: (b, 0, 0)),
            pl.BlockSpec(bias_rows.shape, lambda b, s, pt: (0, 0)),
            pl.BlockSpec((1, d, page), lambda b, s, pt: (b, 0, 0)),
            pl.BlockSpec((1, d, page), lambda b, s, pt: (b, 0, 0)),
            pl.BlockSpec((1, d, page), page_index),
            pl.BlockSpec((1, d, page), page_index),
        ],
        out_specs=pl.BlockSpec((1, n_q, d), lambda b, s, pt: (b, 0, 0)),
        scratch_shapes=[pltpu.VMEM((n_rows, d), BF16), pltpu.VMEM((n_rows, V7X_LANES), F32),
                        pltpu.VMEM((n_rows, d), F32)],
    )
    return pl.pallas_call(
        functools.partial(_sb_paged_kernel, n_q=n_q, page=page),
        grid_spec=grid_spec,
        out_shape=jax.ShapeDtypeStruct((batch, n_q, d), BF16),
        compiler_params=_compiler_params(("parallel", "arbitrary")),
        name="sb_attention_paged",
    )(page_table, q, bias_rows, kt_new_pad, vt_new_pad, cache_kt, cache_vt)


def _prep_weights(norm_mix, norm_mlp, w_in_a, b_i, b_f, head_norm_a, w_out_a, norm_kv, w_kv, w_q_b,
                  w_out_b, sb_bias, w_up, w_down, norm_final):
    qk = MLSTM_HEADS * MLSTM_DK
    vd = MLSTM_HEADS * MLSTM_DV
    w_in = w_in_a[0]
    d = w_in.shape[0]
    gate_w = jnp.zeros((d, V7X_LANES), F32).at[:, :2 * MLSTM_HEADS].set(w_in[:, 2 * qk + 2 * vd:])
    gate_b = jnp.zeros((1, V7X_LANES), F32).at[0, :MLSTM_HEADS].set(b_i[0])
    gate_b = gate_b.at[0, MLSTM_HEADS:2 * MLSTM_HEADS].set(b_f[0])
    row = lambda g: g.reshape(1, -1).astype(F32)
    return dict(
        g_mix0=row(norm_mix[0]), g_mix1=row(norm_mix[1]), g_mlp0=row(norm_mlp[0]), g_mlp1=row(norm_mlp[1]),
        g_kv=row(norm_kv), g_final=row(norm_final), g_head=row(head_norm_a[0]),
        wq=w_in[:, :qk].astype(BF16),
        wkt=(w_in[:, qk:2 * qk] * (MLSTM_DK ** -0.5)).T.astype(BF16),
        wv=w_in[:, 2 * qk:2 * qk + vd].astype(BF16),
        wo=w_in[:, 2 * qk + vd:2 * qk + 2 * vd].astype(BF16),
        wg=gate_w.astype(BF16), gate_b=gate_b,
        w_out_a=w_out_a[0].astype(BF16),
        w_kv=w_kv.astype(BF16),
        w_q=(w_q_b[0] * (SB_HEAD_DIM ** -0.5)).astype(BF16),
        w_out_b=w_out_b[0].astype(BF16),
        sb_bias=sb_bias[0].astype(F32),
        w_up0=w_up[0].astype(BF16), w_up1=w_up[1].astype(BF16),
        w_dn0=w_down[0].astype(BF16), w_dn1=w_down[1].astype(BF16),
    )


def _pack_state(c, n, m):
    pad = jnp.zeros(c.shape[:-1] + (MLSTM_DV - 1,), F32)
    c_aug = jnp.concatenate([c.astype(F32), n.astype(F32)[..., None], pad], axis=-1)
    return c_aug, jnp.broadcast_to(m.astype(F32)[..., None], m.shape + (V7X_LANES,))


def _unpack_state(c_aug, m_rep):
    return c_aug[..., :MLSTM_DV][None], c_aug[..., MLSTM_DV][None], m_rep[..., 0][None]


def _self_decoder(x2d, w, c_aug0, m0, *, batch, valid_len):
    q, kt, v, o, gates = _inproj(x2d, w["g_mix0"], w["wq"], w["wkt"], w["wv"], w["wo"], w["wg"])
    hg, c_aug, m_rep = _mlstm(q, kt, v, o, gates, w["gate_b"], w["g_head"], c_aug0, m0,
                              batch=batch, valid_len=valid_len)
    return hg, c_aug, m_rep


def kernel(x_prompt, x_sample, state_c, state_n, state_m, cache_k, cache_v, page_table, norm_mix, norm_mlp,
           w_in_a, b_i, b_f, head_norm_a, w_out_a, norm_kv, w_kv, w_q_b, w_out_b, sb_bias, w_up, w_down,
           norm_final):
    assert w_in_a.shape[0] == 1 and w_q_b.shape[0] == 1, "one self-decoder and one cross-decoder layer"
    w = _prep_weights(norm_mix, norm_mlp, w_in_a, b_i, b_f, head_norm_a, w_out_a, norm_kv, w_kv, w_q_b,
                      w_out_b, sb_bias, w_up, w_down, norm_final)
    bp, seq, d = x_prompt.shape
    db, dec_seq, _ = x_sample.shape
    n_pages, page = page_table.shape[1], cache_k.shape[1]
    kv_shape = (SB_HEADS, SB_HEAD_DIM)

    xp = x_prompt.reshape(bp * seq, d)
    zeros_c = jnp.zeros((bp, MLSTM_HEADS, MLSTM_DK, 2 * MLSTM_DV), F32)
    zeros_m = jnp.zeros((bp, MLSTM_HEADS, V7X_LANES), F32)
    hg, c_aug_p, m_rep_p = _self_decoder(xp, w, zeros_c, zeros_m, batch=bp, valid_len=MLSTM_CHUNK)
    h = _proj_residual(hg, w["w_out_a"], xp)
    h = _mlp(h, w["g_mlp0"], w["w_up0"], w["w_dn0"], w["g_final"], final_norm=False)
    k_p, v_p, kb, vb, qb = _kvq(h, w["g_kv"], w["g_mix1"], w["w_kv"], w["w_q"])
    attn = _sb_prompt(qb, kb, vb, w["sb_bias"], batch=bp)
    h = _proj_residual(attn, w["w_out_b"], h)
    y_p = _mlp(h, w["g_mlp1"], w["w_up1"], w["w_dn1"], w["g_final"], final_norm=True)
    c_p, n_p, m_p = _unpack_state(c_aug_p, m_rep_p)

    xs = x_sample.reshape(db * dec_seq, d)
    xs_pad = jnp.pad(x_sample, ((0, 0), (0, MLSTM_CHUNK - dec_seq), (0, 0))).reshape(db * MLSTM_CHUNK, d)
    c_aug0, m0 = _pack_state(state_c[0], state_n[0], state_m[0])
    hg_pad, c_aug_s, m_rep_s = _self_decoder(xs_pad, w, c_aug0, m0, batch=db, valid_len=dec_seq)
    hg_s = hg_pad.reshape(db, MLSTM_CHUNK, -1)[:, :dec_seq].reshape(db * dec_seq, -1)
    hs = _proj_residual(hg_s, w["w_out_a"], xs)
    hs = _mlp(hs, w["g_mlp0"], w["w_up0"], w["w_dn0"], w["g_final"], final_norm=False)
    k_s, v_s, kb_s, vb_s, qb_s = _kvq(hs, w["g_kv"], w["g_mix1"], w["w_kv"], w["w_q"])
    new_page = lambda a: jnp.pad(a.reshape(db, dec_seq, d).transpose(0, 2, 1),
                                 ((0, 0), (0, 0), (0, page - dec_seq)))
    feature_major = lambda c: c.transpose(0, 2, 3, 1).reshape(c.shape[0], d, page)
    bias_rows = jnp.broadcast_to(jnp.tile(w["sb_bias"], dec_seq)[:, None], (dec_seq * SB_HEADS, page))
    attn_s = _sb_paged(qb_s.astype(F32).reshape(db, dec_seq, d), bias_rows, new_page(kb_s), new_page(vb_s),
                       feature_major(cache_k), feature_major(cache_v), page_table)
    hs = _proj_residual(attn_s.reshape(db * dec_seq, d), w["w_out_b"], hs)
    y_s = _mlp(hs, w["g_mlp1"], w["w_up1"], w["w_dn1"], w["g_final"], final_norm=True)
    c_s, n_s, m_s = _unpack_state(c_aug_s, m_rep_s)

    return (y_p.reshape(bp, seq, d), y_s.reshape(db, dec_seq, d),
            c_p, n_p, m_p, k_p.reshape(bp, seq, *kv_shape), v_p.reshape(bp, seq, *kv_shape),
            c_s, n_s, m_s, k_s.reshape(db, dec_seq, *kv_shape), v_s.reshape(db, dec_seq, *kv_shape))
```

```python
import functools

import jax
import jax.numpy as jnp
from jax import lax
from jax.experimental import pallas as pl
from jax.experimental.pallas import tpu as pltpu

F32 = jnp.float32
BF16 = jnp.bfloat16

EPS = 1e-6
LOG2E = 1.4426950408889634
MLSTM_HEADS = 8
MLSTM_DK = 64
MLSTM_DV = 128
MLSTM_CHUNK = 128
SB_HEADS = 16
SB_HEAD_DIM = 64

V7X_LANES = 128
V7X_VMEM_BYTES = 64 * 1024 * 1024
VMEM_LIMIT_BYTES = V7X_VMEM_BYTES - 8 * 1024 * 1024

HEADS_PER_LANE_TILE = V7X_LANES // SB_HEAD_DIM
SB_KEY_BLOCK = 256
SB_QUERY_BLOCK = 512
PAGES_PER_STEP = 8
NEG_INF = float("-inf")


def _compiler_params(semantics):
    return pltpu.CompilerParams(dimension_semantics=semantics, vmem_limit_bytes=VMEM_LIMIT_BYTES)


def _dot(a, b):
    return jnp.dot(a, b, preferred_element_type=F32)


def _dot_nt(a, b):
    return lax.dot_general(a, b, (((1,), (1,)), ((), ())), preferred_element_type=F32)


def _dot_exact(a, b):
    return jnp.dot(a, b, preferred_element_type=F32, precision=lax.Precision.HIGHEST)


def _rms_scale(x):
    return lax.rsqrt(jnp.mean(x * x, axis=-1, keepdims=True) + EPS)


def _softplus(z):
    sign_bit = jnp.uint32(0x80000000)
    neg_abs = lax.bitcast_convert_type(lax.bitcast_convert_type(z, jnp.uint32) | sign_bit, F32)
    return jnp.maximum(z, 0.0) + jnp.log(1.0 + jnp.exp2(neg_abs * LOG2E))


def _row_tile(t, target):
    tile = min(t, target)
    assert t % tile == 0, (t, tile)
    return tile


def _inproj_kernel(x_ref, g_ref, wq_ref, wkt_ref, wv_ref, wo_ref, wg_ref,
                   q_ref, kt_ref, v_ref, o_ref, gate_ref):
    x = x_ref[...]
    xb = (x * _rms_scale(x) * g_ref[...]).astype(BF16)
    q_ref[...] = _dot(xb, wq_ref[...]).astype(BF16)
    kt_ref[...] = _dot_nt(wkt_ref[...], xb).astype(BF16)
    v_ref[...] = _dot(xb, wv_ref[...]).astype(BF16)
    o_ref[...] = _dot(xb, wo_ref[...])
    gate_ref[...] = _dot(xb, wg_ref[...])


def _inproj(x, gain, wq, wkt, wv, wo, wg):
    t, d = x.shape
    tm = _row_tile(t, 512)
    qk, vd, gw = wq.shape[1], wv.shape[1], wg.shape[1]
    full = lambda i: (0, 0)
    rows = lambda i: (i, 0)
    return pl.pallas_call(
        _inproj_kernel,
        grid=(t // tm,),
        in_specs=[
            pl.BlockSpec((tm, d), rows),
            pl.BlockSpec((1, d), full),
            pl.BlockSpec((d, qk), full),
            pl.BlockSpec((qk, d), full),
            pl.BlockSpec((d, vd), full),
            pl.BlockSpec((d, vd), full),
            pl.BlockSpec((d, gw), full),
        ],
        out_specs=[
            pl.BlockSpec((tm, qk), rows),
            pl.BlockSpec((qk, tm), lambda i: (0, i)),
            pl.BlockSpec((tm, vd), rows),
            pl.BlockSpec((tm, vd), rows),
            pl.BlockSpec((tm, gw), rows),
        ],
        out_shape=[
            jax.ShapeDtypeStruct((t, qk), BF16),
            jax.ShapeDtypeStruct((qk, t), BF16),
            jax.ShapeDtypeStruct((t, vd), BF16),
            jax.ShapeDtypeStruct((t, vd), F32),
            jax.ShapeDtypeStruct((t, gw), F32),
        ],
        compiler_params=_compiler_params(("parallel",)),
        name="mlstm_inproj",
    )(x, gain, wq, wkt, wv, wo, wg)


def _mlstm_kernel(q_ref, kt_ref, v_ref, o_ref, g_ref, bias_ref, ghead_ref, c0_ref, m0_ref,
                  hg_ref, c_ref, m_ref, rows_scr, *, valid_len):
    n_heads, dk, dv, chunk = MLSTM_HEADS, MLSTM_DK, MLSTM_DV, MLSTM_CHUNK

    @pl.when(pl.program_id(1) == 0)
    def _load_state():
        c_ref[...] = c0_ref[...]
        m_ref[...] = m0_ref[...]

    lane = lax.broadcasted_iota(jnp.int32, (chunk, V7X_LANES), 1)
    sub = lax.broadcasted_iota(jnp.int32, (chunk, V7X_LANES), 0)

    gates = g_ref[...] + bias_ref[...]
    log_f = jnp.minimum(gates, 0.0) - jnp.log1p(jnp.exp(-jnp.abs(gates)))
    if valid_len < chunk:
        valid = sub < valid_len
        log_f = jnp.where(valid, log_f, 0.0)
        gates = jnp.where(valid, gates, NEG_INF)
    is_f_lane = (lane >= n_heads) & (lane < 2 * n_heads)
    col_form = jnp.where(is_f_lane, log_f, jnp.where(lane < n_heads, gates, 0.0))
    row_form = col_form.T

    lower = (lane <= sub).astype(F32)
    upper = (sub <= lane).astype(F32)
    b_cols = _dot_exact(lower, jnp.where(is_f_lane, col_form, 0.0))
    is_f_row = (sub >= n_heads) & (sub < 2 * n_heads)
    b_rows = _dot_exact(jnp.where(is_f_row, row_form, 0.0), upper)
    rows_scr[0:chunk, :] = row_form
    rows_scr[chunk:2 * chunk, :] = b_rows

    causal = lane <= sub
    ones_col = jnp.where(lane == 0, 1.0, 0.0).astype(BF16)

    for h in range(n_heads):
        pair, half = divmod(h, HEADS_PER_LANE_TILE)
        b_col = jnp.sum(jnp.where(lane == n_heads + h, b_cols, 0.0), axis=1, keepdims=True)
        ig_row = rows_scr[h:h + 1, :]
        b_row = rows_scr[chunk + n_heads + h:chunk + n_heads + h + 1, :]
        m_prev = m_ref[0, h:h + 1, :][:, 0:1]

        log_d = jnp.where(causal, b_col - b_row + ig_row, NEG_INF)
        m_inter = b_col + m_prev
        m_t = jnp.maximum(m_inter, jnp.max(log_d, axis=1, keepdims=True))
        w_inter = jnp.exp(m_inter - m_t)
        decay_mat = jnp.exp(log_d - m_t)

        q_pair = q_ref[:, pair * V7X_LANES:(pair + 1) * V7X_LANES].astype(F32)
        q_h = jnp.where(lane // dk == half, q_pair, 0.0).astype(BF16)
        kt_pair = kt_ref[pair * V7X_LANES:(pair + 1) * V7X_LANES, :]
        s = _dot(q_h, kt_pair) * decay_mat

        v_h = v_ref[:, h * dv:(h + 1) * dv]
        c_h = c_ref[0, h]
        c_pair = jnp.concatenate([c_h, c_h], axis=0).astype(BF16)
        q_c = _dot(q_h, c_pair)
        q_n = jnp.sum(jnp.where(lane == 0, q_c[:, dv:], 0.0), axis=1, keepdims=True)
        num = w_inter * q_c[:, :dv] + _dot(s.astype(BF16), v_h)
        den = w_inter * q_n + jnp.sum(s, axis=1, keepdims=True)
        h_val = num / jnp.maximum(jnp.abs(den), jnp.exp(-m_t))

        y = h_val * _rms_scale(h_val) * ghead_ref[:, h * dv:(h + 1) * dv]
        gate_o = jax.nn.sigmoid(o_ref[:, h * dv:(h + 1) * dv])
        hg_ref[:, h * dv:(h + 1) * dv] = (gate_o * y).astype(BF16)

        m_new = m_t[chunk - 1:chunk, :]
        b_last = b_col[chunk - 1:chunk, :]
        w_k = jnp.exp(b_last - b_row + ig_row - m_new)
        decay = jnp.exp(b_last + m_prev - m_new)
        kt_h = kt_pair[half * dk:(half + 1) * dk, :].astype(F32)
        v_aug = jnp.concatenate([v_h, ones_col], axis=1)
        c_ref[0, h] = decay * c_h + _dot((kt_h * w_k).astype(BF16), v_aug)
        m_ref[0, h:h + 1, :] = jnp.broadcast_to(m_new, (1, V7X_LANES))


def _mlstm(q, kt, v, o, gates, gate_bias, ghead, c_aug0, m0, *, batch, valid_len):
    t_total = q.shape[0]
    chunk = MLSTM_CHUNK
    n_chunks = t_total // (batch * chunk)
    qk, vd, gw = q.shape[1], v.shape[1], gates.shape[1]
    rows = lambda b, c: (b * n_chunks + c, 0)
    full = lambda b, c: (0, 0)
    state4 = lambda b, c: (b, 0, 0, 0)
    state3 = lambda b, c: (b, 0, 0)
    return pl.pallas_call(
        functools.partial(_mlstm_kernel, valid_len=valid_len),
        grid=(batch, n_chunks),
        in_specs=[
            pl.BlockSpec((chunk, qk), rows),
            pl.BlockSpec((qk, chunk), lambda b, c: (0, b * n_chunks + c)),
            pl.BlockSpec((chunk, vd), rows),
            pl.BlockSpec((chunk, vd), rows),
            pl.BlockSpec((chunk, gw), rows),
            pl.BlockSpec((1, gw), full),
            pl.BlockSpec((1, vd), full),
            pl.BlockSpec((1,) + c_aug0.shape[1:], state4),
            pl.BlockSpec((1,) + m0.shape[1:], state3),
        ],
        out_specs=[
            pl.BlockSpec((chunk, vd), rows),
            pl.BlockSpec((1,) + c_aug0.shape[1:], state4),
            pl.BlockSpec((1,) + m0.shape[1:], state3),
        ],
        out_shape=[
            jax.ShapeDtypeStruct((t_total, vd), BF16),
            jax.ShapeDtypeStruct(c_aug0.shape, F32),
            jax.ShapeDtypeStruct(m0.shape, F32),
        ],
        scratch_shapes=[pltpu.VMEM((2 * chunk, V7X_LANES), F32)],
        compiler_params=_compiler_params(("parallel", "arbitrary")),
        name="mlstm_chunk",
    )(q, kt, v, o, gates, gate_bias, ghead, c_aug0, m0)


def _proj_residual_kernel(a_ref, w_ref, res_ref, out_ref):
    out_ref[...] = res_ref[...] + _dot(a_ref[...], w_ref[...])


def _proj_residual(a, w, res):
    t, k = a.shape
    n = w.shape[1]
    tm = _row_tile(t, 512)
    rows = lambda i: (i, 0)
    return pl.pallas_call(
        _proj_residual_kernel,
        grid=(t // tm,),
        in_specs=[pl.BlockSpec((tm, k), rows), pl.BlockSpec((k, n), lambda i: (0, 0)),
                  pl.BlockSpec((tm, n), rows)],
        out_specs=pl.BlockSpec((tm, n), rows),
        out_shape=jax.ShapeDtypeStruct((t, n), F32),
        compiler_params=_compiler_params(("parallel",)),
        name="proj_residual",
    )(a, w, res)


def _mlp_kernel(x_ref, g_ref, wup_ref, wdn_ref, gfin_ref, out_ref, xn_scr, acc_scr, *, final_norm):
    j = pl.program_id(1)

    @pl.when(j == 0)
    def _start():
        x = x_ref[...]
        xn_scr[...] = (x * _rms_scale(x) * g_ref[...]).astype(BF16)
        acc_scr[...] = x

    u = _dot(xn_scr[...], wup_ref[...])
    act = jnp.square(jnp.maximum(u, 0.0)).astype(BF16)
    acc_scr[...] += _dot(act, wdn_ref[...])

    @pl.when(j == pl.num_programs(1) - 1)
    def _finish():
        y = acc_scr[...]
        if final_norm:
            y = y * _rms_scale(y) * gfin_ref[...]
        out_ref[...] = y


def _mlp(x, gain, w_up, w_down, gain_final, *, final_norm):
    t, d = x.shape
    ff = w_up.shape[1]
    tm = _row_tile(t, 1024)
    tf = _row_tile(ff, 1024)
    return pl.pallas_call(
        functools.partial(_mlp_kernel, final_norm=final_norm),
        grid=(t // tm, ff // tf),
        in_specs=[
            pl.BlockSpec((tm, d), lambda i, j: (i, 0)),
            pl.BlockSpec((1, d), lambda i, j: (0, 0)),
            pl.BlockSpec((d, tf), lambda i, j: (0, j)),
            pl.BlockSpec((tf, d), lambda i, j: (j, 0)),
            pl.BlockSpec((1, d), lambda i, j: (0, 0)),
        ],
        out_specs=pl.BlockSpec((tm, d), lambda i, j: (i, 0)),
        out_shape=jax.ShapeDtypeStruct((t, d), F32),
        scratch_shapes=[pltpu.VMEM((tm, d), BF16), pltpu.VMEM((tm, d), F32)],
        compiler_params=_compiler_params(("parallel", "arbitrary")),
        name="mlp",
    )(x, gain, w_up, w_down, gain_final)


def _kvq_kernel(h_ref, gkv_ref, gq_ref, wkvt_ref, wq_ref, kt_ref, vt_ref, ktb_ref, vtb_ref, qb_ref):
    h = h_ref[...]
    hn = h * _rms_scale(h)
    kvt = _dot_nt(wkvt_ref[...], (hn * gkv_ref[...]).astype(BF16))
    d = kt_ref.shape[1]
    kt, vt = kvt[:d], kvt[d:]
    kt_ref[0] = kt
    vt_ref[0] = vt
    key_block = ktb_ref.shape[3]
    for blk in range(ktb_ref.shape[1]):
        cols = slice(blk * key_block, (blk + 1) * key_block)
        ktb_ref[0, blk] = kt[:, cols].astype(BF16)
        vtb_ref[0, blk] = vt[:, cols].astype(BF16)
    qb_ref[...] = _dot((hn * gq_ref[...]).astype(BF16), wq_ref[...]).astype(BF16)


def _kvq(h, gain_kv, gain_q, w_kvt, w_q, *, batch, key_block):
    t_total, d = h.shape
    t = t_total // batch
    tm = _row_tile(t, 512)
    assert tm % key_block == 0, (tm, key_block)
    nt = t // tm
    n = w_q.shape[1]
    rows = lambda b, i: (b * nt + i, 0)
    full = lambda b, i: (0, 0)
    feat = pl.BlockSpec((1, n, tm), lambda b, i: (b, 0, i))
    blocks = pl.BlockSpec((1, tm // key_block, n, key_block), lambda b, i: (b, i, 0, 0))
    return pl.pallas_call(
        _kvq_kernel,
        grid=(batch, nt),
        in_specs=[pl.BlockSpec((tm, d), rows), pl.BlockSpec((1, d), full), pl.BlockSpec((1, d), full),
                  pl.BlockSpec((2 * n, d), full), pl.BlockSpec((d, n), full)],
        out_specs=[feat, feat, blocks, blocks, pl.BlockSpec((tm, n), rows)],
        out_shape=[jax.ShapeDtypeStruct((batch, n, t), F32), jax.ShapeDtypeStruct((batch, n, t), F32),
                   jax.ShapeDtypeStruct((batch, t // key_block, n, key_block), BF16),
                   jax.ShapeDtypeStruct((batch, t // key_block, n, key_block), BF16),
                   jax.ShapeDtypeStruct((t_total, n), BF16)],
        compiler_params=_compiler_params(("parallel", "parallel")),
        name="kvq_proj",
    )(h, gain_kv, gain_q, w_kvt, w_q)


def _suffix_matrix(tk, with_total):
    n = tk + (V7X_LANES if with_total else 0)
    j = lax.broadcasted_iota(jnp.int32, (tk, n), 0)
    s = lax.broadcasted_iota(jnp.int32, (tk, n), 1)
    return ((j >= s) | (s >= tk)).astype(BF16)


def _sb_prompt_kernel(bias_ref, q_ref, kt_ref, vt_ref, suffix_ref, out_ref,
                      logw_scr, blocksum_scr, total_scr, acc_scr, *, tq, tk):
    pair = pl.program_id(1)
    i = pl.program_id(2)
    n_rep = tk // V7X_LANES
    heads = range(HEADS_PER_LANE_TILE)
    lane = lax.broadcasted_iota(jnp.int32, (tq, V7X_LANES), 1)
    q_pair = q_ref[...].astype(F32)
    q_heads = [jnp.where(lane // SB_HEAD_DIM == e, q_pair, 0.0).astype(BF16) for e in heads]
    biases = [bias_ref[pair * HEADS_PER_LANE_TILE + e] for e in heads]
    suffix = suffix_ref[...]

    def score_block(j, masked):
        kt = kt_ref[0, j]
        if masked:
            row = lax.broadcasted_iota(jnp.int32, (tq, tk), 0) + i * tq
            col = lax.broadcasted_iota(jnp.int32, (tq, tk), 1) + j * tk
            visible = col < row
        for e in heads:
            z = _dot(q_heads[e], kt) + biases[e]
            sp = _softplus(z)
            if masked:
                sp = jnp.where(visible, sp, 0.0)
            log_w = z - _dot(sp.astype(BF16), suffix)
            if masked:
                log_w = jnp.where(visible, log_w, NEG_INF)
            logw_scr[e] = log_w
            blocksum_scr[e] = jnp.broadcast_to(jnp.sum(sp, axis=1, keepdims=True), (tq, V7X_LANES))

    def weigh_block(j):
        vt = vt_ref[0, j]
        for e in heads:
            total = total_scr[e]
            a = jnp.exp(logw_scr[e] - jnp.concatenate([total] * n_rep, axis=1))
            acc_scr[e] += _dot_nt(a.astype(BF16), vt)
            total_scr[e] = total + blocksum_scr[e]

    total_scr[...] = jnp.zeros_like(total_scr)
    acc_scr[...] = jnp.zeros_like(acc_scr)
    ratio = tq // tk
    newest = (i + 1) * ratio - 1
    score_block(newest, True)
    for m in range(1, ratio):
        weigh_block(newest - m + 1)
        score_block(newest - m, True)

    def body(it, carry):
        j = i * ratio - it
        weigh_block(j)
        score_block(j - 1, False)
        return carry

    lax.fori_loop(0, i * ratio, body, 0)
    weigh_block(0)
    out = acc_scr[0]
    for e in heads[1:]:
        out = jnp.where(lane // SB_HEAD_DIM == e, acc_scr[e], out)
    out_ref[...] = out.astype(BF16)


def _sb_prompt(q, kt_blocks, vt_blocks, bias, *, batch):
    t_total, d = q.shape
    t = t_total // batch
    n_blocks, tk = kt_blocks.shape[1], kt_blocks.shape[3]
    tq = min(t, SB_QUERY_BLOCK)
    assert tq % tk == 0 and t % tq == 0, (t, tq, tk)
    nq = t // tq
    n_pairs = d // V7X_LANES
    kv_spec = pl.BlockSpec((1, n_blocks, V7X_LANES, tk), lambda b, p, i: (b, 0, p, 0))
    head_tiles = lambda lanes: pltpu.VMEM((HEADS_PER_LANE_TILE, tq, lanes), F32)
    return pl.pallas_call(
        functools.partial(_sb_prompt_kernel, tq=tq, tk=tk),
        grid=(batch, n_pairs, nq),
        in_specs=[
            pl.BlockSpec(memory_space=pltpu.SMEM),
            pl.BlockSpec((tq, V7X_LANES), lambda b, p, i: (b * nq + i, p)),
            kv_spec,
            kv_spec,
            pl.BlockSpec((tk, tk), lambda b, p, i: (0, 0)),
        ],
        out_specs=pl.BlockSpec((tq, V7X_LANES), lambda b, p, i: (b * nq + i, p)),
        out_shape=jax.ShapeDtypeStruct((t_total, d), BF16),
        scratch_shapes=[head_tiles(tk), head_tiles(V7X_LANES), head_tiles(V7X_LANES), head_tiles(V7X_LANES)],
        compiler_params=_compiler_params(("parallel", "parallel", "arbitrary")),
        name="sb_attention_prompt",
    )(bias, q, kt_blocks, vt_blocks, _suffix_matrix(tk, with_total=False))


def _sb_paged_kernel(pt_ref, q_ref, bias_ref, knew_ref, vnew_ref, *rest, n_q, page, n_slots):
    del pt_ref
    k_refs, v_refs = rest[:n_slots], rest[n_slots:2 * n_slots]
    out_ref, qbd_scr, total_scr, acc_scr = rest[2 * n_slots:]
    step = pl.program_id(1)
    n_rows = n_q * SB_HEADS
    d = q_ref.shape[2]
    n_rep = page // V7X_LANES
    row_d = lax.broadcasted_iota(jnp.int32, (SB_HEADS, d), 0)
    col_d = lax.broadcasted_iota(jnp.int32, (SB_HEADS, d), 1)
    head_cols = col_d // SB_HEAD_DIM == row_d

    j = lax.broadcasted_iota(jnp.int32, (page, page + V7X_LANES), 0)
    s = lax.broadcasted_iota(jnp.int32, (page, page + V7X_LANES), 1)
    suffix = ((j >= s) | (s >= page)).astype(BF16)

    def visit(pages, visible):
        scored = []
        for kt_page, vt_page in pages:
            z = _dot(qbd_scr[...], kt_page) + bias_ref[...]
            sp = _softplus(z)
            if visible is not None:
                sp = jnp.where(visible, sp, 0.0)
            sums = _dot(sp.astype(BF16), suffix)
            log_w = z - sums[:, :page]
            if visible is not None:
                log_w = jnp.where(visible, log_w, NEG_INF)
            scored.append((log_w, sums[:, page:], vt_page))
        total = total_scr[...]
        acc = acc_scr[...]
        for log_w, page_total, vt_page in scored:
            a = jnp.exp(log_w - jnp.concatenate([total] * n_rep, axis=1))
            acc = acc + _dot_nt(a.astype(BF16), vt_page)
            total = total + page_total
        total_scr[...] = total
        acc_scr[...] = acc

    @pl.when(step == 0)
    def _new_tokens():
        q = q_ref[0]
        blocks = [jnp.where(head_cols, jnp.broadcast_to(q[i:i + 1, :], (SB_HEADS, d)), 0.0)
                  for i in range(n_q)]
        qbd_scr[...] = jnp.concatenate(blocks, axis=0).astype(BF16)
        total_scr[...] = jnp.zeros_like(total_scr)
        acc_scr[...] = jnp.zeros_like(acc_scr)
        q_idx = lax.broadcasted_iota(jnp.int32, (n_rows, page), 0) // SB_HEADS
        key_idx = lax.broadcasted_iota(jnp.int32, (n_rows, page), 1)
        visit([(knew_ref[0], vnew_ref[0])], key_idx < q_idx)

    @pl.when(step > 0)
    def _past_pages():
        visit([(k_refs[p][0].astype(BF16), v_refs[p][0].astype(BF16)) for p in range(n_slots)], None)

    @pl.when(step == pl.num_programs(1) - 1)
    def _emit():
        acc = acc_scr[...]
        rows = []
        for i in range(n_q):
            blk = jnp.where(head_cols, acc[i * SB_HEADS:(i + 1) * SB_HEADS, :], 0.0)
            rows.append(jnp.sum(blk, axis=0, keepdims=True))
        out_ref[0] = jnp.concatenate(rows, axis=0).astype(BF16)


def _sb_paged(q, bias_rows, kt_new_pad, vt_new_pad, cache_kt, cache_vt, page_table, *, n_slots):
    batch, n_q, d = q.shape
    n_pages = page_table.shape[1]
    page = cache_kt.shape[2]
    n_rows = n_q * SB_HEADS
    assert n_pages % n_slots == 0, (n_pages, n_slots)

    def page_spec(slot):
        def index(b, s, pt):
            logical = n_pages - 1 - ((jnp.maximum(s, 1) - 1) * n_slots + slot)
            return (pt[b, logical], 0, 0)
        return pl.BlockSpec((1, d, page), index)

    per_seq = lambda b, s, pt: (b, 0, 0)
    grid_spec = pltpu.PrefetchScalarGridSpec(
        num_scalar_prefetch=1,
        grid=(batch, n_pages // n_slots + 1),
        in_specs=[
            pl.BlockSpec((1, n_q, d), per_seq),
            pl.BlockSpec(bias_rows.shape, lambda b, s, pt: (0, 0)),
            pl.BlockSpec((1, d, page), per_seq),
            pl.BlockSpec((1, d, page), per_seq),
        ] + [page_spec(p) for p in range(n_slots)] * 2,
        out_specs=pl.BlockSpec((1, n_q, d), per_seq),
        scratch_shapes=[pltpu.VMEM((n_rows, d), BF16), pltpu.VMEM((n_rows, V7X_LANES), F32),
                        pltpu.VMEM((n_rows, d), F32)],
    )
    return pl.pallas_call(
        functools.partial(_sb_paged_kernel, n_q=n_q, page=page, n_slots=n_slots),
        grid_spec=grid_spec,
        out_shape=jax.ShapeDtypeStruct((batch, n_q, d), BF16),
        compiler_params=_compiler_params(("parallel", "arbitrary")),
        name="sb_attention_paged",
    )(page_table, q, bias_rows, kt_new_pad, vt_new_pad, *([cache_kt] * n_slots), *([cache_vt] * n_slots))


def _prep_weights(norm_mix, norm_mlp, w_in_a, b_i, b_f, head_norm_a, w_out_a, norm_kv, w_kv, w_q_b,
                  w_out_b, sb_bias, w_up, w_down, norm_final):
    qk = MLSTM_HEADS * MLSTM_DK
    vd = MLSTM_HEADS * MLSTM_DV
    w_in = w_in_a[0]
    d = w_in.shape[0]
    gate_w = jnp.zeros((d, V7X_LANES), F32).at[:, :2 * MLSTM_HEADS].set(w_in[:, 2 * qk + 2 * vd:])
    gate_b = jnp.zeros((1, V7X_LANES), F32).at[0, :MLSTM_HEADS].set(b_i[0])
    gate_b = gate_b.at[0, MLSTM_HEADS:2 * MLSTM_HEADS].set(b_f[0])
    row = lambda g: g.reshape(1, -1).astype(F32)
    return dict(
        g_mix0=row(norm_mix[0]), g_mix1=row(norm_mix[1]), g_mlp0=row(norm_mlp[0]), g_mlp1=row(norm_mlp[1]),
        g_kv=row(norm_kv), g_final=row(norm_final), g_head=row(head_norm_a[0]),
        wq=w_in[:, :qk].astype(BF16),
        wkt=(w_in[:, qk:2 * qk] * (MLSTM_DK ** -0.5)).T.astype(BF16),
        wv=w_in[:, 2 * qk:2 * qk + vd].astype(BF16),
        wo=w_in[:, 2 * qk + vd:2 * qk + 2 * vd].astype(BF16),
        wg=gate_w.astype(BF16), gate_b=gate_b,
        w_out_a=w_out_a[0].astype(BF16),
        w_kvt=w_kv.T.astype(BF16),
        w_q=(w_q_b[0] * (SB_HEAD_DIM ** -0.5)).astype(BF16),
        w_out_b=w_out_b[0].astype(BF16),
        sb_bias=sb_bias[0].astype(F32),
        w_up0=w_up[0].astype(BF16), w_up1=w_up[1].astype(BF16),
        w_dn0=w_down[0].astype(BF16), w_dn1=w_down[1].astype(BF16),
    )


def _pack_state(c, n, m):
    pad = jnp.zeros(c.shape[:-1] + (MLSTM_DV - 1,), F32)
    c_aug = jnp.concatenate([c.astype(F32), n.astype(F32)[..., None], pad], axis=-1)
    return c_aug, jnp.broadcast_to(m.astype(F32)[..., None], m.shape + (V7X_LANES,))


def _unpack_state(c_aug, m_rep):
    return c_aug[..., :MLSTM_DV][None], c_aug[..., MLSTM_DV][None], m_rep[..., 0][None]


def _self_decoder(x2d, w, c_aug0, m0, *, batch, valid_len):
    q, kt, v, o, gates = _inproj(x2d, w["g_mix0"], w["wq"], w["wkt"], w["wv"], w["wo"], w["wg"])
    hg, c_aug, m_rep = _mlstm(q, kt, v, o, gates, w["gate_b"], w["g_head"], c_aug0, m0,
                              batch=batch, valid_len=valid_len)
    return hg, c_aug, m_rep


def kernel(x_prompt, x_sample, state_c, state_n, state_m, cache_k, cache_v, page_table, norm_mix, norm_mlp,
           w_in_a, b_i, b_f, head_norm_a, w_out_a, norm_kv, w_kv, w_q_b, w_out_b, sb_bias, w_up, w_down,
           norm_final):
    assert w_in_a.shape[0] == 1 and w_q_b.shape[0] == 1, "one self-decoder and one cross-decoder layer"
    w = _prep_weights(norm_mix, norm_mlp, w_in_a, b_i, b_f, head_norm_a, w_out_a, norm_kv, w_kv, w_q_b,
                      w_out_b, sb_bias, w_up, w_down, norm_final)
    bp, seq, d = x_prompt.shape
    db, dec_seq, _ = x_sample.shape
    n_pages, page = page_table.shape[1], cache_k.shape[1]
    kv_shape = (SB_HEADS, SB_HEAD_DIM)

    xp = x_prompt.reshape(bp * seq, d)
    zeros_c = jnp.zeros((bp, MLSTM_HEADS, MLSTM_DK, 2 * MLSTM_DV), F32)
    zeros_m = jnp.zeros((bp, MLSTM_HEADS, V7X_LANES), F32)
    hg, c_aug_p, m_rep_p = _self_decoder(xp, w, zeros_c, zeros_m, batch=bp, valid_len=MLSTM_CHUNK)
    h = _proj_residual(hg, w["w_out_a"], xp)
    h = _mlp(h, w["g_mlp0"], w["w_up0"], w["w_dn0"], w["g_final"], final_norm=False)
    kt_p, vt_p, ktb, vtb, qb = _kvq(h, w["g_kv"], w["g_mix1"], w["w_kvt"], w["w_q"],
                                    batch=bp, key_block=min(seq, SB_KEY_BLOCK))
    attn = _sb_prompt(qb, ktb, vtb, w["sb_bias"], batch=bp)
    h = _proj_residual(attn, w["w_out_b"], h)
    y_p = _mlp(h, w["g_mlp1"], w["w_up1"], w["w_dn1"], w["g_final"], final_norm=True)
    c_p, n_p, m_p = _unpack_state(c_aug_p, m_rep_p)
    token_major = lambda a: a.reshape(a.shape[0], *kv_shape, a.shape[2]).transpose(0, 3, 1, 2)

    n_new = db * dec_seq
    xs = x_sample.reshape(n_new, d)
    xs_pad = jnp.pad(x_sample, ((0, 0), (0, MLSTM_CHUNK - dec_seq), (0, 0))).reshape(db * MLSTM_CHUNK, d)
    c_aug0, m0 = _pack_state(state_c[0], state_n[0], state_m[0])
    hg_pad, c_aug_s, m_rep_s = _self_decoder(xs_pad, w, c_aug0, m0, batch=db, valid_len=dec_seq)
    hg_s = hg_pad.reshape(db, MLSTM_CHUNK, -1)[:, :dec_seq].reshape(n_new, -1)
    hs = _proj_residual(hg_s, w["w_out_a"], xs)
    hs = _mlp(hs, w["g_mlp0"], w["w_up0"], w["w_dn0"], w["g_final"], final_norm=False)
    kt_s, vt_s, ktb_s, vtb_s, qb_s = _kvq(hs, w["g_kv"], w["g_mix1"], w["w_kvt"], w["w_q"],
                                          batch=1, key_block=n_new)
    new_page = lambda a: jnp.pad(a.reshape(d, db, dec_seq).transpose(1, 0, 2),
                                 ((0, 0), (0, 0), (0, page - dec_seq)))
    feature_major = lambda c: c.transpose(0, 2, 3, 1).reshape(c.shape[0], d, page)
    bias_rows = jnp.broadcast_to(jnp.tile(w["sb_bias"], dec_seq)[:, None], (dec_seq * SB_HEADS, page))
    attn_s = _sb_paged(qb_s.astype(F32).reshape(db, dec_seq, d), bias_rows, new_page(ktb_s), new_page(vtb_s),
                       feature_major(cache_k), feature_major(cache_v), page_table,
                       n_slots=min(n_pages, PAGES_PER_STEP))
    hs = _proj_residual(attn_s.reshape(n_new, d), w["w_out_b"], hs)
    y_s = _mlp(hs, w["g_mlp1"], w["w_up1"], w["w_dn1"], w["g_final"], final_norm=True)
    c_s, n_s, m_s = _unpack_state(c_aug_s, m_rep_s)
    new_rows = lambda a: a.reshape(*kv_shape, db, dec_seq).transpose(2, 3, 0, 1)

    return (y_p.reshape(bp, seq, d), y_s.reshape(db, dec_seq, d),
            c_p, n_p, m_p, token_major(kt_p), token_major(vt_p),
            c_s, n_s, m_s, new_rows(kt_s), new_rows(vt_s))
```

```python
import functools

import jax
import jax.numpy as jnp
from jax import lax
from jax.experimental import pallas as pl
from jax.experimental.pallas import tpu as pltpu

F32 = jnp.float32
BF16 = jnp.bfloat16

EPS = 1e-6
LOG2E = 1.4426950408889634
MLSTM_HEADS = 8
MLSTM_DK = 64
MLSTM_DV = 128
MLSTM_CHUNK = 128
SB_HEADS = 16
SB_HEAD_DIM = 64

V7X_LANES = 128
V7X_VMEM_BYTES = 64 * 1024 * 1024
VMEM_LIMIT_BYTES = V7X_VMEM_BYTES - 8 * 1024 * 1024

HEADS_PER_LANE_TILE = V7X_LANES // SB_HEAD_DIM
SB_KEY_BLOCK = 256
SB_QUERY_BLOCK = 1024
PAGES_PER_STEP = 8
NEG_INF = float("-inf")


def _compiler_params(semantics):
    return pltpu.CompilerParams(dimension_semantics=semantics, vmem_limit_bytes=VMEM_LIMIT_BYTES)


def _dot(a, b):
    return jnp.dot(a, b, preferred_element_type=F32)


def _dot_nt(a, b):
    return lax.dot_general(a, b, (((1,), (1,)), ((), ())), preferred_element_type=F32)


def _dot_exact(a, b):
    return jnp.dot(a, b, preferred_element_type=F32, precision=lax.Precision.HIGHEST)


def _rms_scale(x):
    return lax.rsqrt(jnp.mean(x * x, axis=-1, keepdims=True) + EPS)


def _softplus(z):
    return jnp.maximum(z, 0.0) + jnp.log(1.0 + jnp.exp2(jnp.abs(z) * (-LOG2E)))


def _row_tile(t, target):
    tile = min(t, target)
    assert t % tile == 0, (t, tile)
    return tile


def _inproj_kernel(x_ref, g_ref, wq_ref, wkt_ref, wv_ref, wo_ref, wg_ref,
                   q_ref, kt_ref, v_ref, o_ref, gate_ref):
    x = x_ref[...]
    xb = (x * _rms_scale(x) * g_ref[...]).astype(BF16)
    q_ref[...] = _dot(xb, wq_ref[...]).astype(BF16)
    kt_ref[...] = _dot_nt(wkt_ref[...], xb).astype(BF16)
    v_ref[...] = _dot(xb, wv_ref[...]).astype(BF16)
    o_ref[...] = _dot(xb, wo_ref[...])
    gate_ref[...] = _dot(xb, wg_ref[...])


def _inproj(x, gain, wq, wkt, wv, wo, wg):
    t, d = x.shape
    tm = _row_tile(t, 512)
    qk, vd, gw = wq.shape[1], wv.shape[1], wg.shape[1]
    full = lambda i: (0, 0)
    rows = lambda i: (i, 0)
    return pl.pallas_call(
        _inproj_kernel,
        grid=(t // tm,),
        in_specs=[
            pl.BlockSpec((tm, d), rows),
            pl.BlockSpec((1, d), full),
            pl.BlockSpec((d, qk), full),
            pl.BlockSpec((qk, d), full),
            pl.BlockSpec((d, vd), full),
            pl.BlockSpec((d, vd), full),
            pl.BlockSpec((d, gw), full),
        ],
        out_specs=[
            pl.BlockSpec((tm, qk), rows),
            pl.BlockSpec((qk, tm), lambda i: (0, i)),
            pl.BlockSpec((tm, vd), rows),
            pl.BlockSpec((tm, vd), rows),
            pl.BlockSpec((tm, gw), rows),
        ],
        out_shape=[
            jax.ShapeDtypeStruct((t, qk), BF16),
            jax.ShapeDtypeStruct((qk, t), BF16),
            jax.ShapeDtypeStruct((t, vd), BF16),
            jax.ShapeDtypeStruct((t, vd), F32),
            jax.ShapeDtypeStruct((t, gw), F32),
        ],
        compiler_params=_compiler_params(("parallel",)),
        name="mlstm_inproj",
    )(x, gain, wq, wkt, wv, wo, wg)


def _mlstm_kernel(q_ref, kt_ref, v_ref, o_ref, g_ref, bias_ref, ghead_ref, c0_ref, m0_ref,
                  hg_ref, c_ref, m_ref, rows_scr, *, valid_len):
    n_heads, dk, dv, chunk = MLSTM_HEADS, MLSTM_DK, MLSTM_DV, MLSTM_CHUNK

    @pl.when(pl.program_id(1) == 0)
    def _load_state():
        c_ref[...] = c0_ref[...]
        m_ref[...] = m0_ref[...]

    lane = lax.broadcasted_iota(jnp.int32, (chunk, V7X_LANES), 1)
    sub = lax.broadcasted_iota(jnp.int32, (chunk, V7X_LANES), 0)

    gates = g_ref[...] + bias_ref[...]
    log_f = jnp.minimum(gates, 0.0) - jnp.log1p(jnp.exp(-jnp.abs(gates)))
    if valid_len < chunk:
        valid = sub < valid_len
        log_f = jnp.where(valid, log_f, 0.0)
        gates = jnp.where(valid, gates, NEG_INF)
    is_f_lane = (lane >= n_heads) & (lane < 2 * n_heads)
    col_form = jnp.where(is_f_lane, log_f, jnp.where(lane < n_heads, gates, 0.0))
    row_form = col_form.T

    lower = (lane <= sub).astype(F32)
    upper = (sub <= lane).astype(F32)
    b_cols = _dot_exact(lower, jnp.where(is_f_lane, col_form, 0.0))
    is_f_row = (sub >= n_heads) & (sub < 2 * n_heads)
    b_rows = _dot_exact(jnp.where(is_f_row, row_form, 0.0), upper)
    rows_scr[0:chunk, :] = row_form
    rows_scr[chunk:2 * chunk, :] = b_rows

    causal = lane <= sub
    ones_block = jnp.ones((chunk, dv), BF16)

    for h in range(n_heads):
        pair, half = divmod(h, HEADS_PER_LANE_TILE)
        b_col = jnp.sum(jnp.where(lane == n_heads + h, b_cols, 0.0), axis=1, keepdims=True)
        ig_row = rows_scr[h:h + 1, :]
        b_row = rows_scr[chunk + n_heads + h:chunk + n_heads + h + 1, :]
        m_prev = m_ref[0, h:h + 1, :][:, 0:1]

        log_d = jnp.where(causal, b_col - b_row + ig_row, NEG_INF)
        m_inter = b_col + m_prev
        m_t = jnp.maximum(m_inter, jnp.max(log_d, axis=1, keepdims=True))
        w_inter = jnp.exp(m_inter - m_t)
        decay_mat = jnp.exp(log_d - m_t)

        q_pair = q_ref[:, pair * V7X_LANES:(pair + 1) * V7X_LANES].astype(F32)
        q_h = jnp.where(lane // dk == half, q_pair, 0.0).astype(BF16)
        kt_pair = kt_ref[pair * V7X_LANES:(pair + 1) * V7X_LANES, :]
        s = _dot(q_h, kt_pair) * decay_mat

        v_aug = jnp.concatenate([v_ref[:, h * dv:(h + 1) * dv], ones_block], axis=1)
        c_h = c_ref[0, h]
        c_pair = jnp.concatenate([c_h, c_h], axis=0).astype(BF16)
        q_c = _dot(q_h, c_pair)
        s_v = _dot(s.astype(BF16), v_aug)
        num = w_inter * q_c[:, :dv] + s_v[:, :dv]
        den = w_inter * q_c[:, dv:] + s_v[:, dv:]
        h_val = num / jnp.maximum(jnp.abs(den), jnp.exp(-m_t))

        y = h_val * _rms_scale(h_val) * ghead_ref[:, h * dv:(h + 1) * dv]
        gate_o = jax.nn.sigmoid(o_ref[:, h * dv:(h + 1) * dv])
        hg_ref[:, h * dv:(h + 1) * dv] = (gate_o * y).astype(BF16)

        m_new = m_t[chunk - 1:chunk, :]
        b_last = b_col[chunk - 1:chunk, :]
        w_k = jnp.exp(b_last - b_row + ig_row - m_new)
        decay = jnp.exp(b_last + m_prev - m_new)
        kt_h = kt_pair[half * dk:(half + 1) * dk, :].astype(F32)
        c_ref[0, h] = decay * c_h + _dot((kt_h * w_k).astype(BF16), v_aug)
        m_ref[0, h:h + 1, :] = jnp.broadcast_to(m_new, (1, V7X_LANES))


def _mlstm(q, kt, v, o, gates, gate_bias, ghead, c_aug0, m0, *, batch, valid_len):
    t_total = q.shape[0]
    chunk = MLSTM_CHUNK
    n_chunks = t_total // (batch * chunk)
    qk, vd, gw = q.shape[1], v.shape[1], gates.shape[1]
    rows = lambda b, c: (b * n_chunks + c, 0)
    full = lambda b, c: (0, 0)
    state4 = lambda b, c: (b, 0, 0, 0)
    state3 = lambda b, c: (b, 0, 0)
    return pl.pallas_call(
        functools.partial(_mlstm_kernel, valid_len=valid_len),
        grid=(batch, n_chunks),
        in_specs=[
            pl.BlockSpec((chunk, qk), rows),
            pl.BlockSpec((qk, chunk), lambda b, c: (0, b * n_chunks + c)),
            pl.BlockSpec((chunk, vd), rows),
            pl.BlockSpec((chunk, vd), rows),
            pl.BlockSpec((chunk, gw), rows),
            pl.BlockSpec((1, gw), full),
            pl.BlockSpec((1, vd), full),
            pl.BlockSpec((1,) + c_aug0.shape[1:], state4),
            pl.BlockSpec((1,) + m0.shape[1:], state3),
        ],
        out_specs=[
            pl.BlockSpec((chunk, vd), rows),
            pl.BlockSpec((1,) + c_aug0.shape[1:], state4),
            pl.BlockSpec((1,) + m0.shape[1:], state3),
        ],
        out_shape=[
            jax.ShapeDtypeStruct((t_total, vd), BF16),
            jax.ShapeDtypeStruct(c_aug0.shape, F32),
            jax.ShapeDtypeStruct(m0.shape, F32),
        ],
        scratch_shapes=[pltpu.VMEM((2 * chunk, V7X_LANES), F32)],
        compiler_params=_compiler_params(("parallel", "arbitrary")),
        name="mlstm_chunk",
    )(q, kt, v, o, gates, gate_bias, ghead, c_aug0, m0)


def _layer_tail_kernel(mix_ref, wout_ref, res_ref, g_ref, wup_ref, wdn_ref, gfin_ref, out_ref,
                       hn_scr, acc_scr, *, final_norm):
    j = pl.program_id(1)

    @pl.when(j == 0)
    def _start():
        h = res_ref[...] + _dot(mix_ref[...], wout_ref[...])
        hn_scr[...] = (h * _rms_scale(h) * g_ref[...]).astype(BF16)
        acc_scr[...] = h

    u = _dot(hn_scr[...], wup_ref[...])
    act = jnp.square(jnp.maximum(u, 0.0)).astype(BF16)
    acc_scr[...] += _dot(act, wdn_ref[...])

    @pl.when(j == pl.num_programs(1) - 1)
    def _finish():
        y = acc_scr[...]
        if final_norm:
            y = y * _rms_scale(y) * gfin_ref[...]
        out_ref[...] = y


def _layer_tail(mix, w_out, res, gain, w_up, w_down, gain_final, *, final_norm):
    t, d = res.shape
    k = mix.shape[1]
    ff = w_up.shape[1]
    tm = _row_tile(t, 1024)
    tf = _row_tile(ff, 1024)
    rows = lambda i, j: (i, 0)
    full = lambda i, j: (0, 0)
    return pl.pallas_call(
        functools.partial(_layer_tail_kernel, final_norm=final_norm),
        grid=(t // tm, ff // tf),
        in_specs=[
            pl.BlockSpec((tm, k), rows),
            pl.BlockSpec((k, d), full),
            pl.BlockSpec((tm, d), rows),
            pl.BlockSpec((1, d), full),
            pl.BlockSpec((d, tf), lambda i, j: (0, j)),
            pl.BlockSpec((tf, d), lambda i, j: (j, 0)),
            pl.BlockSpec((1, d), full),
        ],
        out_specs=pl.BlockSpec((tm, d), rows),
        out_shape=jax.ShapeDtypeStruct((t, d), F32),
        scratch_shapes=[pltpu.VMEM((tm, d), BF16), pltpu.VMEM((tm, d), F32)],
        compiler_params=_compiler_params(("parallel", "arbitrary")),
        name="layer_tail",
    )(mix, w_out, res, gain, w_up, w_down, gain_final)


def _kvq_kernel(h_ref, gkv_ref, gq_ref, wkvt_ref, wq_ref, kt_ref, vt_ref, ktb_ref, vtb_ref, qb_ref):
    h = h_ref[...]
    hn = h * _rms_scale(h)
    kvt = _dot_nt(wkvt_ref[...], (hn * gkv_ref[...]).astype(BF16))
    d = kt_ref.shape[1]
    kt, vt = kvt[:d], kvt[d:]
    kt_ref[0] = kt
    vt_ref[0] = vt
    key_block = ktb_ref.shape[3]
    for blk in range(ktb_ref.shape[1]):
        cols = slice(blk * key_block, (blk + 1) * key_block)
        ktb_ref[0, blk] = kt[:, cols].astype(BF16)
        vtb_ref[0, blk] = vt[:, cols].astype(BF16)
    qb_ref[...] = _dot((hn * gq_ref[...]).astype(BF16), wq_ref[...]).astype(BF16)


def _kvq(h, gain_kv, gain_q, w_kvt, w_q, *, batch, key_block):
    t_total, d = h.shape
    t = t_total // batch
    tm = _row_tile(t, 512)
    assert tm % key_block == 0, (tm, key_block)
    nt = t // tm
    n = w_q.shape[1]
    rows = lambda b, i: (b * nt + i, 0)
    full = lambda b, i: (0, 0)
    feat = pl.BlockSpec((1, n, tm), lambda b, i: (b, 0, i))
    blocks = pl.BlockSpec((1, tm // key_block, n, key_block), lambda b, i: (b, i, 0, 0))
    return pl.pallas_call(
        _kvq_kernel,
        grid=(batch, nt),
        in_specs=[pl.BlockSpec((tm, d), rows), pl.BlockSpec((1, d), full), pl.BlockSpec((1, d), full),
                  pl.BlockSpec((2 * n, d), full), pl.BlockSpec((d, n), full)],
        out_specs=[feat, feat, blocks, blocks, pl.BlockSpec((tm, n), rows)],
        out_shape=[jax.ShapeDtypeStruct((batch, n, t), F32), jax.ShapeDtypeStruct((batch, n, t), F32),
                   jax.ShapeDtypeStruct((batch, t // key_block, n, key_block), BF16),
                   jax.ShapeDtypeStruct((batch, t // key_block, n, key_block), BF16),
                   jax.ShapeDtypeStruct((t_total, n), BF16)],
        compiler_params=_compiler_params(("parallel", "parallel")),
        name="kvq_proj",
    )(h, gain_kv, gain_q, w_kvt, w_q)


def _suffix_matrix(tk, with_total):
    n = tk + (V7X_LANES if with_total else 0)
    j = lax.broadcasted_iota(jnp.int32, (tk, n), 0)
    s = lax.broadcasted_iota(jnp.int32, (tk, n), 1)
    return ((j >= s) | (s >= tk)).astype(BF16)


def _sb_prompt_kernel(bias_ref, q_ref, kt_ref, vt_ref, suffix_ref, out_ref,
                      logw_scr, blocksum_scr, total_scr, acc_scr, *, tq, tk):
    pair = pl.program_id(1)
    i = pl.program_id(2)
    n_rep = tk // V7X_LANES
    heads = range(HEADS_PER_LANE_TILE)
    lane = lax.broadcasted_iota(jnp.int32, (tq, V7X_LANES), 1)
    q_pair = q_ref[...].astype(F32)
    q_heads = [jnp.where(lane // SB_HEAD_DIM == e, q_pair, 0.0).astype(BF16) for e in heads]
    biases = [bias_ref[pair * HEADS_PER_LANE_TILE + e] for e in heads]
    suffix = suffix_ref[...]

    def score_rows(j, r0, r1, diagonal):
        kt = kt_ref[0, j]
        if diagonal:
            row = lax.broadcasted_iota(jnp.int32, (r1 - r0, tk), 0)
            col = lax.broadcasted_iota(jnp.int32, (r1 - r0, tk), 1)
            visible = col < row
        for e in heads:
            z = _dot(q_heads[e][r0:r1], kt) + biases[e]
            sp = _softplus(z)
            if diagonal:
                sp = jnp.where(visible, sp, 0.0)
            log_w = z - _dot(sp.astype(BF16), suffix)
            if diagonal:
                log_w = jnp.where(visible, log_w, NEG_INF)
            logw_scr[e, r0:r1, :] = log_w
            blocksum_scr[e, r0:r1, :] = jnp.broadcast_to(jnp.sum(sp, axis=1, keepdims=True),
                                                         (r1 - r0, V7X_LANES))

    def weigh_rows(j, r0):
        vt = vt_ref[0, j]
        for e in heads:
            total = total_scr[e, r0:, :]
            a = jnp.exp(logw_scr[e, r0:, :] - jnp.concatenate([total] * n_rep, axis=1))
            acc_scr[e, r0:, :] += _dot_nt(a.astype(BF16), vt)
            total_scr[e, r0:, :] = total + blocksum_scr[e, r0:, :]

    total_scr[...] = jnp.zeros_like(total_scr)
    acc_scr[...] = jnp.zeros_like(acc_scr)
    ratio = tq // tk
    base = i * ratio

    def score_diagonal(d):
        score_rows(base + d, d * tk, (d + 1) * tk, True)
        if (d + 1) * tk < tq:
            score_rows(base + d, (d + 1) * tk, tq, False)

    score_diagonal(ratio - 1)
    for d in range(ratio - 1, 0, -1):
        weigh_rows(base + d, d * tk)
        score_diagonal(d - 1)

    def body(it, carry):
        j = base - it
        weigh_rows(j, 0)
        score_rows(j - 1, 0, tq, False)
        return carry

    lax.fori_loop(0, base, body, 0)
    weigh_rows(0, 0)
    out = acc_scr[0]
    for e in heads[1:]:
        out = jnp.where(lane // SB_HEAD_DIM == e, acc_scr[e], out)
    out_ref[...] = out.astype(BF16)


def _sb_prompt(q, kt_blocks, vt_blocks, bias, *, batch):
    t_total, d = q.shape
    t = t_total // batch
    n_blocks, tk = kt_blocks.shape[1], kt_blocks.shape[3]
    tq = min(t, SB_QUERY_BLOCK)
    assert tq % tk == 0 and t % tq == 0, (t, tq, tk)
    nq = t // tq
    n_pairs = d // V7X_LANES
    kv_spec = pl.BlockSpec((1, n_blocks, V7X_LANES, tk), lambda b, p, i: (b, 0, p, 0))
    head_tiles = lambda lanes: pltpu.VMEM((HEADS_PER_LANE_TILE, tq, lanes), F32)
    return pl.pallas_call(
        functools.partial(_sb_prompt_kernel, tq=tq, tk=tk),
        grid=(batch, n_pairs, nq),
        in_specs=[
            pl.BlockSpec(memory_space=pltpu.SMEM),
            pl.BlockSpec((tq, V7X_LANES), lambda b, p, i: (b * nq + i, p)),
            kv_spec,
            kv_spec,
            pl.BlockSpec((tk, tk), lambda b, p, i: (0, 0)),
        ],
        out_specs=pl.BlockSpec((tq, V7X_LANES), lambda b, p, i: (b * nq + i, p)),
        out_shape=jax.ShapeDtypeStruct((t_total, d), BF16),
        scratch_shapes=[head_tiles(tk), head_tiles(V7X_LANES), head_tiles(V7X_LANES), head_tiles(V7X_LANES)],
        compiler_params=_compiler_params(("parallel", "parallel", "arbitrary")),
        name="sb_attention_prompt",
    )(bias, q, kt_blocks, vt_blocks, _suffix_matrix(tk, with_total=False))


def _sb_paged_kernel(pt_ref, q_ref, bias_ref, knew_ref, vnew_ref, *rest, n_q, page, n_slots):
    del pt_ref
    k_refs, v_refs = rest[:n_slots], rest[n_slots:2 * n_slots]
    out_ref, qbd_scr, total_scr, acc_scr = rest[2 * n_slots:]
    step = pl.program_id(1)
    n_rows = n_q * SB_HEADS
    d = q_ref.shape[2]
    n_rep = page // V7X_LANES
    row_d = lax.broadcasted_iota(jnp.int32, (SB_HEADS, d), 0)
    col_d = lax.broadcasted_iota(jnp.int32, (SB_HEADS, d), 1)
    head_cols = col_d // SB_HEAD_DIM == row_d

    j = lax.broadcasted_iota(jnp.int32, (page, page + V7X_LANES), 0)
    s = lax.broadcasted_iota(jnp.int32, (page, page + V7X_LANES), 1)
    suffix = ((j >= s) | (s >= page)).astype(BF16)

    def visit(pages, visible):
        scored = []
        for kt_page, vt_page in pages:
            z = _dot(qbd_scr[...], kt_page) + bias_ref[...]
            sp = _softplus(z)
            if visible is not None:
                sp = jnp.where(visible, sp, 0.0)
            sums = _dot(sp.astype(BF16), suffix)
            log_w = z - sums[:, :page]
            if visible is not None:
                log_w = jnp.where(visible, log_w, NEG_INF)
            scored.append((log_w, sums[:, page:], vt_page))
        total = total_scr[...]
        acc = acc_scr[...]
        for log_w, page_total, vt_page in scored:
            a = jnp.exp(log_w - jnp.concatenate([total] * n_rep, axis=1))
            acc = acc + _dot_nt(a.astype(BF16), vt_page)
            total = total + page_total
        total_scr[...] = total
        acc_scr[...] = acc

    @pl.when(step == 0)
    def _new_tokens():
        q = q_ref[0]
        blocks = [jnp.where(head_cols, jnp.broadcast_to(q[i:i + 1, :], (SB_HEADS, d)), 0.0)
                  for i in range(n_q)]
        qbd_scr[...] = jnp.concatenate(blocks, axis=0).astype(BF16)
        total_scr[...] = jnp.zeros_like(total_scr)
        acc_scr[...] = jnp.zeros_like(acc_scr)
        q_idx = lax.broadcasted_iota(jnp.int32, (n_rows, page), 0) // SB_HEADS
        key_idx = lax.broadcasted_iota(jnp.int32, (n_rows, page), 1)
        visit([(knew_ref[0], vnew_ref[0])], key_idx < q_idx)

    @pl.when(step > 0)
    def _past_pages():
        visit([(k_refs[p][0].astype(BF16), v_refs[p][0].astype(BF16)) for p in range(n_slots)], None)

    @pl.when(step == pl.num_programs(1) - 1)
    def _emit():
        acc = acc_scr[...]
        rows = []
        for i in range(n_q):
            blk = jnp.where(head_cols, acc[i * SB_HEADS:(i + 1) * SB_HEADS, :], 0.0)
            rows.append(jnp.sum(blk, axis=0, keepdims=True))
        out_ref[0] = jnp.concatenate(rows, axis=0).astype(BF16)


def _sb_paged(q, bias_rows, kt_new_pad, vt_new_pad, cache_kt, cache_vt, page_table, *, n_slots):
    batch, n_q, d = q.shape
    n_pages = page_table.shape[1]
    page = cache_kt.shape[2]
    n_rows = n_q * SB_HEADS
    assert n_pages % n_slots == 0, (n_pages, n_slots)

    def page_spec(slot):
        def index(b, s, pt):
            logical = n_pages - 1 - ((jnp.maximum(s, 1) - 1) * n_slots + slot)
            return (pt[b, logical], 0, 0)
        return pl.BlockSpec((1, d, page), index)

    per_seq = lambda b, s, pt: (b, 0, 0)
    grid_spec = pltpu.PrefetchScalarGridSpec(
        num_scalar_prefetch=1,
        grid=(batch, n_pages // n_slots + 1),
        in_specs=[
            pl.BlockSpec((1, n_q, d), per_seq),
            pl.BlockSpec(bias_rows.shape, lambda b, s, pt: (0, 0)),
            pl.BlockSpec((1, d, page), per_seq),
            pl.BlockSpec((1, d, page), per_seq),
        ] + [page_spec(p) for p in range(n_slots)] * 2,
        out_specs=pl.BlockSpec((1, n_q, d), per_seq),
        scratch_shapes=[pltpu.VMEM((n_rows, d), BF16), pltpu.VMEM((n_rows, V7X_LANES), F32),
                        pltpu.VMEM((n_rows, d), F32)],
    )
    return pl.pallas_call(
        functools.partial(_sb_paged_kernel, n_q=n_q, page=page, n_slots=n_slots),
        grid_spec=grid_spec,
        out_shape=jax.ShapeDtypeStruct((batch, n_q, d), BF16),
        compiler_params=_compiler_params(("parallel", "arbitrary")),
        name="sb_attention_paged",
    )(page_table, q, bias_rows, kt_new_pad, vt_new_pad, *([cache_kt] * n_slots), *([cache_vt] * n_slots))


def _prep_weights(norm_mix, norm_mlp, w_in_a, b_i, b_f, head_norm_a, w_out_a, norm_kv, w_kv, w_q_b,
                  w_out_b, sb_bias, w_up, w_down, norm_final):
    qk = MLSTM_HEADS * MLSTM_DK
    vd = MLSTM_HEADS * MLSTM_DV
    w_in = w_in_a[0]
    d = w_in.shape[0]
    gate_w = jnp.zeros((d, V7X_LANES), F32).at[:, :2 * MLSTM_HEADS].set(w_in[:, 2 * qk + 2 * vd:])
    gate_b = jnp.zeros((1, V7X_LANES), F32).at[0, :MLSTM_HEADS].set(b_i[0])
    gate_b = gate_b.at[0, MLSTM_HEADS:2 * MLSTM_HEADS].set(b_f[0])
    row = lambda g: g.reshape(1, -1).astype(F32)
    return dict(
        g_mix0=row(norm_mix[0]), g_mix1=row(norm_mix[1]), g_mlp0=row(norm_mlp[0]), g_mlp1=row(norm_mlp[1]),
        g_kv=row(norm_kv), g_final=row(norm_final), g_head=row(head_norm_a[0]),
        wq=w_in[:, :qk].astype(BF16),
        wkt=(w_in[:, qk:2 * qk] * (MLSTM_DK ** -0.5)).T.astype(BF16),
        wv=w_in[:, 2 * qk:2 * qk + vd].astype(BF16),
        wo=w_in[:, 2 * qk + vd:2 * qk + 2 * vd].astype(BF16),
        wg=gate_w.astype(BF16), gate_b=gate_b,
        w_out_a=w_out_a[0].astype(BF16),
        w_kvt=w_kv.T.astype(BF16),
        w_q=(w_q_b[0] * (SB_HEAD_DIM ** -0.5)).astype(BF16),
        w_out_b=w_out_b[0].astype(BF16),
        sb_bias=sb_bias[0].astype(F32),
        w_up0=w_up[0].astype(BF16), w_up1=w_up[1].astype(BF16),
        w_dn0=w_down[0].astype(BF16), w_dn1=w_down[1].astype(BF16),
    )


def _pack_state(c, n, m):
    n_rep = jnp.broadcast_to(n.astype(F32)[..., None], n.shape + (MLSTM_DV,))
    c_aug = jnp.concatenate([c.astype(F32), n_rep], axis=-1)
    return c_aug, jnp.broadcast_to(m.astype(F32)[..., None], m.shape + (V7X_LANES,))


def _unpack_state(c_aug, m_rep):
    return c_aug[..., :MLSTM_DV][None], c_aug[..., MLSTM_DV][None], m_rep[..., 0][None]


def _self_decoder(x2d, w, c_aug0, m0, *, batch, valid_len):
    q, kt, v, o, gates = _inproj(x2d, w["g_mix0"], w["wq"], w["wkt"], w["wv"], w["wo"], w["wg"])
    hg, c_aug, m_rep = _mlstm(q, kt, v, o, gates, w["gate_b"], w["g_head"], c_aug0, m0,
                              batch=batch, valid_len=valid_len)
    return hg, c_aug, m_rep


def kernel(x_prompt, x_sample, state_c, state_n, state_m, cache_k, cache_v, page_table, norm_mix, norm_mlp,
           w_in_a, b_i, b_f, head_norm_a, w_out_a, norm_kv, w_kv, w_q_b, w_out_b, sb_bias, w_up, w_down,
           norm_final):
    assert w_in_a.shape[0] == 1 and w_q_b.shape[0] == 1, "one self-decoder and one cross-decoder layer"
    w = _prep_weights(norm_mix, norm_mlp, w_in_a, b_i, b_f, head_norm_a, w_out_a, norm_kv, w_kv, w_q_b,
                      w_out_b, sb_bias, w_up, w_down, norm_final)
    bp, seq, d = x_prompt.shape
    db, dec_seq, _ = x_sample.shape
    n_pages, page = page_table.shape[1], cache_k.shape[1]
    kv_shape = (SB_HEADS, SB_HEAD_DIM)

    xp = x_prompt.reshape(bp * seq, d)
    zeros_c = jnp.zeros((bp, MLSTM_HEADS, MLSTM_DK, 2 * MLSTM_DV), F32)
    zeros_m = jnp.zeros((bp, MLSTM_HEADS, V7X_LANES), F32)
    hg, c_aug_p, m_rep_p = _self_decoder(xp, w, zeros_c, zeros_m, batch=bp, valid_len=MLSTM_CHUNK)
    h = _layer_tail(hg, w["w_out_a"], xp, w["g_mlp0"], w["w_up0"], w["w_dn0"], w["g_final"], final_norm=False)
    kt_p, vt_p, ktb, vtb, qb = _kvq(h, w["g_kv"], w["g_mix1"], w["w_kvt"], w["w_q"],
                                    batch=bp, key_block=min(seq, SB_KEY_BLOCK))
    attn = _sb_prompt(qb, ktb, vtb, w["sb_bias"], batch=bp)
    y_p = _layer_tail(attn, w["w_out_b"], h, w["g_mlp1"], w["w_up1"], w["w_dn1"], w["g_final"], final_norm=True)
    c_p, n_p, m_p = _unpack_state(c_aug_p, m_rep_p)
    token_major = lambda a: a.reshape(a.shape[0], *kv_shape, a.shape[2]).transpose(0, 3, 1, 2)

    n_new = db * dec_seq
    xs = x_sample.reshape(n_new, d)
    xs_pad = jnp.pad(x_sample, ((0, 0), (0, MLSTM_CHUNK - dec_seq), (0, 0))).reshape(db * MLSTM_CHUNK, d)
    c_aug0, m0 = _pack_state(state_c[0], state_n[0], state_m[0])
    hg_pad, c_aug_s, m_rep_s = _self_decoder(xs_pad, w, c_aug0, m0, batch=db, valid_len=dec_seq)
    hg_s = hg_pad.reshape(db, MLSTM_CHUNK, -1)[:, :dec_seq].reshape(n_new, -1)
    hs = _layer_tail(hg_s, w["w_out_a"], xs, w["g_mlp0"], w["w_up0"], w["w_dn0"], w["g_final"], final_norm=False)
    kt_s, vt_s, ktb_s, vtb_s, qb_s = _kvq(hs, w["g_kv"], w["g_mix1"], w["w_kvt"], w["w_q"],
                                          batch=1, key_block=n_new)
    new_page = lambda a: jnp.pad(a.reshape(d, db, dec_seq).transpose(1, 0, 2),
                                 ((0, 0), (0, 0), (0, page - dec_seq)))
    feature_major = lambda c: c.transpose(0, 2, 3, 1).reshape(c.shape[0], d, page)
    bias_rows = jnp.broadcast_to(jnp.tile(w["sb_bias"], dec_seq)[:, None], (dec_seq * SB_HEADS, page))
    attn_s = _sb_paged(qb_s.astype(F32).reshape(db, dec_seq, d), bias_rows, new_page(ktb_s), new_page(vtb_s),
                       feature_major(cache_k), feature_major(cache_v), page_table,
                       n_slots=min(n_pages, PAGES_PER_STEP))
    y_s = _layer_tail(attn_s.reshape(n_new, d), w["w_out_b"], hs, w["g_mlp1"], w["w_up1"], w["w_dn1"],
                      w["g_final"], final_norm=True)
    c_s, n_s, m_s = _unpack_state(c_aug_s, m_rep_s)
    new_rows = lambda a: a.reshape(*kv_shape, db, dec_seq).transpose(2, 3, 0, 1)

    return (y_p.reshape(bp, seq, d), y_s.reshape(db, dec_seq, d),
            c_p, n_p, m_p, token_major(kt_p), token_major(vt_p),
            c_s, n_s, m_s, new_rows(kt_s), new_rows(vt_s))
```

```python
import functools
import math

import jax
import jax.numpy as jnp
from jax import lax
from jax.experimental import pallas as pl
from jax.experimental.pallas import tpu as pltpu

F32 = jnp.float32
BF16 = jnp.bfloat16

EPS = 1e-6
LOG2E = 1.4426950408889634
MLSTM_HEADS = 8
MLSTM_DK = 64
MLSTM_DV = 128
MLSTM_CHUNK = 128
SB_HEADS = 16
SB_HEAD_DIM = 64

V7X_LANES = 128
V7X_VMEM_BYTES = 64 * 1024 * 1024
VMEM_LIMIT_BYTES = V7X_VMEM_BYTES - 8 * 1024 * 1024

HEADS_PER_LANE_TILE = V7X_LANES // SB_HEAD_DIM
SB_KEY_BLOCK = 256
SB_QUERY_BLOCK = 1024
SB_LOOP_UNROLL = 4
PAGES_PER_STEP = 16
NEG_INF = float("-inf")


def _compiler_params(semantics):
    return pltpu.CompilerParams(dimension_semantics=semantics, vmem_limit_bytes=VMEM_LIMIT_BYTES)


def _dot(a, b):
    return jnp.dot(a, b, preferred_element_type=F32)


def _dot_nt(a, b):
    return lax.dot_general(a, b, (((1,), (1,)), ((), ())), preferred_element_type=F32)


def _dot_exact(a, b):
    return jnp.dot(a, b, preferred_element_type=F32, precision=lax.Precision.HIGHEST)


def _rms_scale(x):
    return lax.rsqrt(jnp.mean(x * x, axis=-1, keepdims=True) + EPS)


def _softplus(z):
    return jnp.maximum(z, 0.0) + jnp.log(1.0 + jnp.exp2(jnp.abs(z) * (-LOG2E)))


def _row_tile(t, target):
    tile = min(t, target)
    assert t % tile == 0, (t, tile)
    return tile


def _inproj_kernel(x_ref, g_ref, wq_ref, wkt_ref, wv_ref, wo_ref, wg_ref,
                   q_ref, kt_ref, v_ref, o_ref, gate_ref):
    x = x_ref[...]
    xb = (x * _rms_scale(x) * g_ref[...]).astype(BF16)
    q_ref[...] = _dot(xb, wq_ref[...]).astype(BF16)
    kt_ref[...] = _dot_nt(wkt_ref[...], xb).astype(BF16)
    v_ref[...] = _dot(xb, wv_ref[...]).astype(BF16)
    o_ref[...] = _dot(xb, wo_ref[...])
    gate_ref[...] = _dot(xb, wg_ref[...])


def _inproj(x, gain, wq, wkt, wv, wo, wg):
    t, d = x.shape
    tm = _row_tile(t, 512)
    qk, vd, gw = wq.shape[1], wv.shape[1], wg.shape[1]
    full = lambda i: (0, 0)
    rows = lambda i: (i, 0)
    return pl.pallas_call(
        _inproj_kernel,
        grid=(t // tm,),
        in_specs=[
            pl.BlockSpec((tm, d), rows),
            pl.BlockSpec((1, d), full),
            pl.BlockSpec((d, qk), full),
            pl.BlockSpec((qk, d), full),
            pl.BlockSpec((d, vd), full),
            pl.BlockSpec((d, vd), full),
            pl.BlockSpec((d, gw), full),
        ],
        out_specs=[
            pl.BlockSpec((tm, qk), rows),
            pl.BlockSpec((qk, tm), lambda i: (0, i)),
            pl.BlockSpec((tm, vd), rows),
            pl.BlockSpec((tm, vd), rows),
            pl.BlockSpec((tm, gw), rows),
        ],
        out_shape=[
            jax.ShapeDtypeStruct((t, qk), BF16),
            jax.ShapeDtypeStruct((qk, t), BF16),
            jax.ShapeDtypeStruct((t, vd), BF16),
            jax.ShapeDtypeStruct((t, vd), F32),
            jax.ShapeDtypeStruct((t, gw), F32),
        ],
        compiler_params=_compiler_params(("parallel",)),
        name="mlstm_inproj",
    )(x, gain, wq, wkt, wv, wo, wg)


def _mlstm_kernel(q_ref, kt_ref, v_ref, o_ref, g_ref, bias_ref, ghead_ref, c0_ref, m0_ref,
                  hg_ref, c_ref, m_ref, rows_scr, *, valid_len):
    n_heads, dk, dv, chunk = MLSTM_HEADS, MLSTM_DK, MLSTM_DV, MLSTM_CHUNK

    @pl.when(pl.program_id(1) == 0)
    def _load_state():
        c_ref[...] = c0_ref[...]
        m_ref[...] = m0_ref[...]

    lane = lax.broadcasted_iota(jnp.int32, (chunk, V7X_LANES), 1)
    sub = lax.broadcasted_iota(jnp.int32, (chunk, V7X_LANES), 0)

    gates = g_ref[...] + bias_ref[...]
    log_f = jnp.minimum(gates, 0.0) - jnp.log1p(jnp.exp(-jnp.abs(gates)))
    if valid_len < chunk:
        valid = sub < valid_len
        log_f = jnp.where(valid, log_f, 0.0)
        gates = jnp.where(valid, gates, NEG_INF)
    is_f_lane = (lane >= n_heads) & (lane < 2 * n_heads)
    col_form = jnp.where(is_f_lane, log_f, jnp.where(lane < n_heads, gates, 0.0))
    row_form = col_form.T

    lower = (lane <= sub).astype(F32)
    upper = (sub <= lane).astype(F32)
    b_cols = _dot_exact(lower, jnp.where(is_f_lane, col_form, 0.0))
    is_f_row = (sub >= n_heads) & (sub < 2 * n_heads)
    b_rows = _dot_exact(jnp.where(is_f_row, row_form, 0.0), upper)
    rows_scr[0:chunk, :] = row_form
    rows_scr[chunk:2 * chunk, :] = b_rows

    causal = lane <= sub
    ones_block = jnp.ones((chunk, dv), BF16)

    for h in range(n_heads):
        pair, half = divmod(h, HEADS_PER_LANE_TILE)
        b_col = jnp.sum(jnp.where(lane == n_heads + h, b_cols, 0.0), axis=1, keepdims=True)
        ig_row = rows_scr[h:h + 1, :]
        b_row = rows_scr[chunk + n_heads + h:chunk + n_heads + h + 1, :]
        m_prev = m_ref[0, h:h + 1, :][:, 0:1]

        log_d = jnp.where(causal, b_col - b_row + ig_row, NEG_INF)
        m_inter = b_col + m_prev
        m_t = jnp.maximum(m_inter, jnp.max(log_d, axis=1, keepdims=True))
        w_inter = jnp.exp(m_inter - m_t)
        decay_mat = jnp.exp(log_d - m_t)

        q_pair = q_ref[:, pair * V7X_LANES:(pair + 1) * V7X_LANES].astype(F32)
        q_h = jnp.where(lane // dk == half, q_pair, 0.0).astype(BF16)
        kt_pair = kt_ref[pair * V7X_LANES:(pair + 1) * V7X_LANES, :]
        s = _dot(q_h, kt_pair) * decay_mat

        v_aug = jnp.concatenate([v_ref[:, h * dv:(h + 1) * dv], ones_block], axis=1)
        c_h = c_ref[0, h]
        c_pair = jnp.concatenate([c_h, c_h], axis=0).astype(BF16)
        q_c = _dot(q_h, c_pair)
        s_v = _dot(s.astype(BF16), v_aug)
        num = w_inter * q_c[:, :dv] + s_v[:, :dv]
        den = w_inter * q_c[:, dv:] + s_v[:, dv:]
        h_val = num / jnp.maximum(jnp.abs(den), jnp.exp(-m_t))

        y = h_val * _rms_scale(h_val) * ghead_ref[:, h * dv:(h + 1) * dv]
        gate_o = jax.nn.sigmoid(o_ref[:, h * dv:(h + 1) * dv])
        hg_ref[:, h * dv:(h + 1) * dv] = (gate_o * y).astype(BF16)

        m_new = m_t[chunk - 1:chunk, :]
        b_last = b_col[chunk - 1:chunk, :]
        w_k = jnp.exp(b_last - b_row + ig_row - m_new)
        decay = jnp.exp(b_last + m_prev - m_new)
        kt_h = kt_pair[half * dk:(half + 1) * dk, :].astype(F32)
        c_ref[0, h] = decay * c_h + _dot((kt_h * w_k).astype(BF16), v_aug)
        m_ref[0, h:h + 1, :] = jnp.broadcast_to(m_new, (1, V7X_LANES))


def _mlstm(q, kt, v, o, gates, gate_bias, ghead, c_aug0, m0, *, batch, valid_len):
    t_total = q.shape[0]
    chunk = MLSTM_CHUNK
    n_chunks = t_total // (batch * chunk)
    qk, vd, gw = q.shape[1], v.shape[1], gates.shape[1]
    rows = lambda b, c: (b * n_chunks + c, 0)
    full = lambda b, c: (0, 0)
    state4 = lambda b, c: (b, 0, 0, 0)
    state3 = lambda b, c: (b, 0, 0)
    return pl.pallas_call(
        functools.partial(_mlstm_kernel, valid_len=valid_len),
        grid=(batch, n_chunks),
        in_specs=[
            pl.BlockSpec((chunk, qk), rows),
            pl.BlockSpec((qk, chunk), lambda b, c: (0, b * n_chunks + c)),
            pl.BlockSpec((chunk, vd), rows),
            pl.BlockSpec((chunk, vd), rows),
            pl.BlockSpec((chunk, gw), rows),
            pl.BlockSpec((1, gw), full),
            pl.BlockSpec((1, vd), full),
            pl.BlockSpec((1,) + c_aug0.shape[1:], state4),
            pl.BlockSpec((1,) + m0.shape[1:], state3),
        ],
        out_specs=[
            pl.BlockSpec((chunk, vd), rows),
            pl.BlockSpec((1,) + c_aug0.shape[1:], state4),
            pl.BlockSpec((1,) + m0.shape[1:], state3),
        ],
        out_shape=[
            jax.ShapeDtypeStruct((t_total, vd), BF16),
            jax.ShapeDtypeStruct(c_aug0.shape, F32),
            jax.ShapeDtypeStruct(m0.shape, F32),
        ],
        scratch_shapes=[pltpu.VMEM((2 * chunk, V7X_LANES), F32)],
        compiler_params=_compiler_params(("parallel", "arbitrary")),
        name="mlstm_chunk",
    )(q, kt, v, o, gates, gate_bias, ghead, c_aug0, m0)


def _layer_tail_kernel(mix_ref, wout_ref, res_ref, g_ref, wup_ref, wdn_ref, gfin_ref, out_ref,
                       hn_scr, acc_scr, *, final_norm):
    j = pl.program_id(1)

    @pl.when(j == 0)
    def _start():
        h = res_ref[...] + _dot(mix_ref[...], wout_ref[...])
        hn_scr[...] = (h * _rms_scale(h) * g_ref[...]).astype(BF16)
        acc_scr[...] = h

    u = _dot(hn_scr[...], wup_ref[...])
    act = jnp.square(jnp.maximum(u, 0.0)).astype(BF16)
    acc_scr[...] += _dot(act, wdn_ref[...])

    @pl.when(j == pl.num_programs(1) - 1)
    def _finish():
        y = acc_scr[...]
        if final_norm:
            y = y * _rms_scale(y) * gfin_ref[...]
        out_ref[...] = y


def _layer_tail(mix, w_out, res, gain, w_up, w_down, gain_final, *, final_norm):
    t, d = res.shape
    k = mix.shape[1]
    ff = w_up.shape[1]
    tm = _row_tile(t, 1024)
    tf = _row_tile(ff, 1024)
    rows = lambda i, j: (i, 0)
    full = lambda i, j: (0, 0)
    return pl.pallas_call(
        functools.partial(_layer_tail_kernel, final_norm=final_norm),
        grid=(t // tm, ff // tf),
        in_specs=[
            pl.BlockSpec((tm, k), rows),
            pl.BlockSpec((k, d), full),
            pl.BlockSpec((tm, d), rows),
            pl.BlockSpec((1, d), full),
            pl.BlockSpec((d, tf), lambda i, j: (0, j)),
            pl.BlockSpec((tf, d), lambda i, j: (j, 0)),
            pl.BlockSpec((1, d), full),
        ],
        out_specs=pl.BlockSpec((tm, d), rows),
        out_shape=jax.ShapeDtypeStruct((t, d), F32),
        scratch_shapes=[pltpu.VMEM((tm, d), BF16), pltpu.VMEM((tm, d), F32)],
        compiler_params=_compiler_params(("parallel", "arbitrary")),
        name="layer_tail",
    )(mix, w_out, res, gain, w_up, w_down, gain_final)


def _kvq_kernel(h_ref, gkv_ref, gq_ref, wkvt_ref, wq_ref, kt_ref, vt_ref, ktb_ref, vtb_ref, qb_ref):
    h = h_ref[...]
    hn = h * _rms_scale(h)
    kvt = _dot_nt(wkvt_ref[...], (hn * gkv_ref[...]).astype(BF16))
    d = kt_ref.shape[1]
    kt, vt = kvt[:d], kvt[d:]
    kt_ref[0] = kt
    vt_ref[0] = vt
    key_block = ktb_ref.shape[3]
    for blk in range(ktb_ref.shape[1]):
        cols = slice(blk * key_block, (blk + 1) * key_block)
        ktb_ref[0, blk] = kt[:, cols].astype(BF16)
        vtb_ref[0, blk] = vt[:, cols].astype(BF16)
    qb_ref[...] = _dot((hn * gq_ref[...]).astype(BF16), wq_ref[...]).astype(BF16)


def _kvq(h, gain_kv, gain_q, w_kvt, w_q, *, batch, key_block):
    t_total, d = h.shape
    t = t_total // batch
    tm = _row_tile(t, 512)
    assert tm % key_block == 0, (tm, key_block)
    nt = t // tm
    n = w_q.shape[1]
    rows = lambda b, i: (b * nt + i, 0)
    full = lambda b, i: (0, 0)
    feat = pl.BlockSpec((1, n, tm), lambda b, i: (b, 0, i))
    blocks = pl.BlockSpec((1, tm // key_block, n, key_block), lambda b, i: (b, i, 0, 0))
    return pl.pallas_call(
        _kvq_kernel,
        grid=(batch, nt),
        in_specs=[pl.BlockSpec((tm, d), rows), pl.BlockSpec((1, d), full), pl.BlockSpec((1, d), full),
                  pl.BlockSpec((2 * n, d), full), pl.BlockSpec((d, n), full)],
        out_specs=[feat, feat, blocks, blocks, pl.BlockSpec((tm, n), rows)],
        out_shape=[jax.ShapeDtypeStruct((batch, n, t), F32), jax.ShapeDtypeStruct((batch, n, t), F32),
                   jax.ShapeDtypeStruct((batch, t // key_block, n, key_block), BF16),
                   jax.ShapeDtypeStruct((batch, t // key_block, n, key_block), BF16),
                   jax.ShapeDtypeStruct((t_total, n), BF16)],
        compiler_params=_compiler_params(("parallel", "parallel")),
        name="kvq_proj",
    )(h, gain_kv, gain_q, w_kvt, w_q)


def _suffix_matrix(tk, with_total):
    n = tk + (V7X_LANES if with_total else 0)
    j = lax.broadcasted_iota(jnp.int32, (tk, n), 0)
    s = lax.broadcasted_iota(jnp.int32, (tk, n), 1)
    return ((j >= s) | (s >= tk)).astype(BF16)


def _sb_prompt_kernel(bias_ref, q_ref, kt_ref, vt_ref, suffix_ref, out_ref,
                      logw_scr, blocksum_scr, total_scr, acc_scr, *, tq, tk):
    pair = pl.program_id(1)
    i = pl.program_id(2)
    n_rep = tk // V7X_LANES
    heads = range(HEADS_PER_LANE_TILE)
    lane = lax.broadcasted_iota(jnp.int32, (tq, V7X_LANES), 1)
    q_pair = q_ref[...].astype(F32)
    q_heads = [jnp.where(lane // SB_HEAD_DIM == e, q_pair, 0.0).astype(BF16) for e in heads]
    biases = [bias_ref[pair * HEADS_PER_LANE_TILE + e] for e in heads]
    suffix = suffix_ref[...]

    def score_rows(j, slot, r0, r1, diagonal):
        kt = kt_ref[0, j]
        if diagonal:
            row = lax.broadcasted_iota(jnp.int32, (r1 - r0, tk), 0)
            col = lax.broadcasted_iota(jnp.int32, (r1 - r0, tk), 1)
            visible = col < row
        for e in heads:
            z = _dot(q_heads[e][r0:r1], kt) + biases[e]
            sp = _softplus(z)
            if diagonal:
                sp = jnp.where(visible, sp, 0.0)
            log_w = z - _dot(sp.astype(BF16), suffix)
            if diagonal:
                log_w = jnp.where(visible, log_w, NEG_INF)
            logw_scr[slot, e, r0:r1, :] = log_w
            blocksum_scr[slot, e, r0:r1, :] = jnp.broadcast_to(jnp.sum(sp, axis=1, keepdims=True),
                                                               (r1 - r0, V7X_LANES))

    def weigh_rows(j, slot, r0):
        vt = vt_ref[0, j]
        for e in heads:
            total = total_scr[e, r0:, :]
            a = jnp.exp(logw_scr[slot, e, r0:, :] - jnp.concatenate([total] * n_rep, axis=1))
            acc_scr[e, r0:, :] += _dot_nt(a.astype(BF16), vt)
            total_scr[e, r0:, :] = total + blocksum_scr[slot, e, r0:, :]

    total_scr[...] = jnp.zeros_like(total_scr)
    acc_scr[...] = jnp.zeros_like(acc_scr)
    ratio = tq // tk
    base = i * ratio
    n_slots = logw_scr.shape[0]
    assert ratio % n_slots == 0

    def score_diagonal(d):
        score_rows(base + d, d % n_slots, d * tk, (d + 1) * tk, True)
        if (d + 1) * tk < tq:
            score_rows(base + d, d % n_slots, (d + 1) * tk, tq, False)

    score_diagonal(ratio - 1)
    for d in range(ratio - 1, 0, -1):
        weigh_rows(base + d, d % n_slots, d * tk)
        score_diagonal(d - 1)

    def body(it, carry):
        j = base - it * n_slots
        for u in range(n_slots):
            weigh_rows(j - u, (-u) % n_slots, 0)
            score_rows(j - u - 1, (-u - 1) % n_slots, 0, tq, False)
        return carry

    lax.fori_loop(0, base // n_slots, body, 0)
    weigh_rows(0, 0, 0)
    out = acc_scr[0]
    for e in heads[1:]:
        out = jnp.where(lane // SB_HEAD_DIM == e, acc_scr[e], out)
    out_ref[...] = out.astype(BF16)


def _sb_prompt(q, kt_blocks, vt_blocks, bias, *, batch):
    t_total, d = q.shape
    t = t_total // batch
    n_blocks, tk = kt_blocks.shape[1], kt_blocks.shape[3]
    tq = min(t, SB_QUERY_BLOCK)
    assert tq % tk == 0 and t % tq == 0, (t, tq, tk)
    nq = t // tq
    n_pairs = d // V7X_LANES
    kv_spec = pl.BlockSpec((1, n_blocks, V7X_LANES, tk), lambda b, p, i: (b, 0, p, 0))
    head_tiles = lambda lanes: pltpu.VMEM((HEADS_PER_LANE_TILE, tq, lanes), F32)
    n_slots = math.gcd(tq // tk, SB_LOOP_UNROLL)
    slot_tiles = lambda lanes: pltpu.VMEM((n_slots, HEADS_PER_LANE_TILE, tq, lanes), F32)
    return pl.pallas_call(
        functools.partial(_sb_prompt_kernel, tq=tq, tk=tk),
        grid=(batch, n_pairs, nq),
        in_specs=[
            pl.BlockSpec(memory_space=pltpu.SMEM),
            pl.BlockSpec((tq, V7X_LANES), lambda b, p, i: (b * nq + i, p)),
            kv_spec,
            kv_spec,
            pl.BlockSpec((tk, tk), lambda b, p, i: (0, 0)),
        ],
        out_specs=pl.BlockSpec((tq, V7X_LANES), lambda b, p, i: (b * nq + i, p)),
        out_shape=jax.ShapeDtypeStruct((t_total, d), BF16),
        scratch_shapes=[slot_tiles(tk), slot_tiles(V7X_LANES), head_tiles(V7X_LANES), head_tiles(V7X_LANES)],
        compiler_params=_compiler_params(("parallel", "parallel", "arbitrary")),
        name="sb_attention_prompt",
    )(bias, q, kt_blocks, vt_blocks, _suffix_matrix(tk, with_total=False))


def _sb_paged_kernel(pt_ref, q_ref, bias_ref, knew_ref, vnew_ref, *rest, n_q, page, n_slots):
    del pt_ref
    k_refs, v_refs = rest[:n_slots], rest[n_slots:2 * n_slots]
    out_ref, qbd_scr, total_scr, acc_scr = rest[2 * n_slots:]
    step = pl.program_id(1)
    n_rows = n_q * SB_HEADS
    d = q_ref.shape[2]
    n_rep = page // V7X_LANES
    row_d = lax.broadcasted_iota(jnp.int32, (SB_HEADS, d), 0)
    col_d = lax.broadcasted_iota(jnp.int32, (SB_HEADS, d), 1)
    head_cols = col_d // SB_HEAD_DIM == row_d

    j = lax.broadcasted_iota(jnp.int32, (page, page + V7X_LANES), 0)
    s = lax.broadcasted_iota(jnp.int32, (page, page + V7X_LANES), 1)
    suffix = ((j >= s) | (s >= page)).astype(BF16)

    def visit(pages, visible):
        scored = []
        for kt_page, vt_page in pages:
            z = _dot(qbd_scr[...], kt_page) + bias_ref[...]
            sp = _softplus(z)
            if visible is not None:
                sp = jnp.where(visible, sp, 0.0)
            sums = _dot(sp.astype(BF16), suffix)
            log_w = z - sums[:, :page]
            if visible is not None:
                log_w = jnp.where(visible, log_w, NEG_INF)
            scored.append((log_w, sums[:, page:], vt_page))
        total = total_scr[...]
        acc = acc_scr[...]
        for log_w, page_total, vt_page in scored:
            a = jnp.exp(log_w - jnp.concatenate([total] * n_rep, axis=1))
            acc = acc + _dot_nt(a.astype(BF16), vt_page)
            total = total + page_total
        total_scr[...] = total
        acc_scr[...] = acc

    @pl.when(step == 0)
    def _new_tokens():
        q = q_ref[0]
        blocks = [jnp.where(head_cols, jnp.broadcast_to(q[i:i + 1, :], (SB_HEADS, d)), 0.0)
                  for i in range(n_q)]
        qbd_scr[...] = jnp.concatenate(blocks, axis=0).astype(BF16)
        total_scr[...] = jnp.zeros_like(total_scr)
        acc_scr[...] = jnp.zeros_like(acc_scr)
        q_idx = lax.broadcasted_iota(jnp.int32, (n_rows, page), 0) // SB_HEADS
        key_idx = lax.broadcasted_iota(jnp.int32, (n_rows, page), 1)
        visit([(knew_ref[0], vnew_ref[0])], key_idx < q_idx)

    @pl.when(step > 0)
    def _past_pages():
        visit([(k_refs[p][0].astype(BF16), v_refs[p][0].astype(BF16)) for p in range(n_slots)], None)

    @pl.when(step == pl.num_programs(1) - 1)
    def _emit():
        acc = acc_scr[...]
        rows = []
        for i in range(n_q):
            blk = jnp.where(head_cols, acc[i * SB_HEADS:(i + 1) * SB_HEADS, :], 0.0)
            rows.append(jnp.sum(blk, axis=0, keepdims=True))
        out_ref[0] = jnp.concatenate(rows, axis=0).astype(BF16)


def _sb_paged(q, bias_rows, kt_new_pad, vt_new_pad, cache_kt, cache_vt, page_table, *, n_slots):
    batch, n_q, d = q.shape
    n_pages = page_table.shape[1]
    page = cache_kt.shape[2]
    n_rows = n_q * SB_HEADS
    assert n_pages % n_slots == 0, (n_pages, n_slots)

    def page_spec(slot):
        def index(b, s, pt):
            logical = n_pages - 1 - ((jnp.maximum(s, 1) - 1) * n_slots + slot)
            return (pt[b, logical], 0, 0)
        return pl.BlockSpec((1, d, page), index)

    per_seq = lambda b, s, pt: (b, 0, 0)
    grid_spec = pltpu.PrefetchScalarGridSpec(
        num_scalar_prefetch=1,
        grid=(batch, n_pages // n_slots + 1),
        in_specs=[
            pl.BlockSpec((1, n_q, d), per_seq),
            pl.BlockSpec(bias_rows.shape, lambda b, s, pt: (0, 0)),
            pl.BlockSpec((1, d, page), per_seq),
            pl.BlockSpec((1, d, page), per_seq),
        ] + [page_spec(p) for p in range(n_slots)] * 2,
        out_specs=pl.BlockSpec((1, n_q, d), per_seq),
        scratch_shapes=[pltpu.VMEM((n_rows, d), BF16), pltpu.VMEM((n_rows, V7X_LANES), F32),
                        pltpu.VMEM((n_rows, d), F32)],
    )
    return pl.pallas_call(
        functools.partial(_sb_paged_kernel, n_q=n_q, page=page, n_slots=n_slots),
        grid_spec=grid_spec,
        out_shape=jax.ShapeDtypeStruct((batch, n_q, d), BF16),
        compiler_params=_compiler_params(("parallel", "arbitrary")),
        name="sb_attention_paged",
    )(page_table, q, bias_rows, kt_new_pad, vt_new_pad, *([cache_kt] * n_slots), *([cache_vt] * n_slots))


def _prep_weights(norm_mix, norm_mlp, w_in_a, b_i, b_f, head_norm_a, w_out_a, norm_kv, w_kv, w_q_b,
                  w_out_b, sb_bias, w_up, w_down, norm_final):
    qk = MLSTM_HEADS * MLSTM_DK
    vd = MLSTM_HEADS * MLSTM_DV
    w_in = w_in_a[0]
    d = w_in.shape[0]
    gate_w = jnp.zeros((d, V7X_LANES), F32).at[:, :2 * MLSTM_HEADS].set(w_in[:, 2 * qk + 2 * vd:])
    gate_b = jnp.zeros((1, V7X_LANES), F32).at[0, :MLSTM_HEADS].set(b_i[0])
    gate_b = gate_b.at[0, MLSTM_HEADS:2 * MLSTM_HEADS].set(b_f[0])
    row = lambda g: g.reshape(1, -1).astype(F32)
    return dict(
        g_mix0=row(norm_mix[0]), g_mix1=row(norm_mix[1]), g_mlp0=row(norm_mlp[0]), g_mlp1=row(norm_mlp[1]),
        g_kv=row(norm_kv), g_final=row(norm_final), g_head=row(head_norm_a[0]),
        wq=w_in[:, :qk].astype(BF16),
        wkt=(w_in[:, qk:2 * qk] * (MLSTM_DK ** -0.5)).T.astype(BF16),
        wv=w_in[:, 2 * qk:2 * qk + vd].astype(BF16),
        wo=w_in[:, 2 * qk + vd:2 * qk + 2 * vd].astype(BF16),
        wg=gate_w.astype(BF16), gate_b=gate_b,
        w_out_a=w_out_a[0].astype(BF16),
        w_kvt=w_kv.T.astype(BF16),
        w_q=(w_q_b[0] * (SB_HEAD_DIM ** -0.5)).astype(BF16),
        w_out_b=w_out_b[0].astype(BF16),
        sb_bias=sb_bias[0].astype(F32),
        w_up0=w_up[0].astype(BF16), w_up1=w_up[1].astype(BF16),
        w_dn0=w_down[0].astype(BF16), w_dn1=w_down[1].astype(BF16),
    )


def _pack_state(c, n, m):
    n_rep = jnp.broadcast_to(n.astype(F32)[..., None], n.shape + (MLSTM_DV,))
    c_aug = jnp.concatenate([c.astype(F32), n_rep], axis=-1)
    return c_aug, jnp.broadcast_to(m.astype(F32)[..., None], m.shape + (V7X_LANES,))


def _unpack_state(c_aug, m_rep):
    return c_aug[..., :MLSTM_DV][None], c_aug[..., MLSTM_DV][None], m_rep[..., 0][None]


def _self_decoder(x2d, w, c_aug0, m0, *, batch, valid_len):
    q, kt, v, o, gates = _inproj(x2d, w["g_mix0"], w["wq"], w["wkt"], w["wv"], w["wo"], w["wg"])
    hg, c_aug, m_rep = _mlstm(q, kt, v, o, gates, w["gate_b"], w["g_head"], c_aug0, m0,
                              batch=batch, valid_len=valid_len)
    return hg, c_aug, m_rep


def kernel(x_prompt, x_sample, state_c, state_n, state_m, cache_k, cache_v, page_table, norm_mix, norm_mlp,
           w_in_a, b_i, b_f, head_norm_a, w_out_a, norm_kv, w_kv, w_q_b, w_out_b, sb_bias, w_up, w_down,
           norm_final):
    assert w_in_a.shape[0] == 1 and w_q_b.shape[0] == 1, "one self-decoder and one cross-decoder layer"
    w = _prep_weights(norm_mix, norm_mlp, w_in_a, b_i, b_f, head_norm_a, w_out_a, norm_kv, w_kv, w_q_b,
                      w_out_b, sb_bias, w_up, w_down, norm_final)
    bp, seq, d = x_prompt.shape
    db, dec_seq, _ = x_sample.shape
    n_pages, page = page_table.shape[1], cache_k.shape[1]
    kv_shape = (SB_HEADS, SB_HEAD_DIM)

    xp = x_prompt.reshape(bp * seq, d)
    zeros_c = jnp.zeros((bp, MLSTM_HEADS, MLSTM_DK, 2 * MLSTM_DV), F32)
    zeros_m = jnp.zeros((bp, MLSTM_HEADS, V7X_LANES), F32)
    hg, c_aug_p, m_rep_p = _self_decoder(xp, w, zeros_c, zeros_m, batch=bp, valid_len=MLSTM_CHUNK)
    h = _layer_tail(hg, w["w_out_a"], xp, w["g_mlp0"], w["w_up0"], w["w_dn0"], w["g_final"], final_norm=False)
    kt_p, vt_p, ktb, vtb, qb = _kvq(h, w["g_kv"], w["g_mix1"], w["w_kvt"], w["w_q"],
                                    batch=bp, key_block=min(seq, SB_KEY_BLOCK))
    attn = _sb_prompt(qb, ktb, vtb, w["sb_bias"], batch=bp)
    y_p = _layer_tail(attn, w["w_out_b"], h, w["g_mlp1"], w["w_up1"], w["w_dn1"], w["g_final"], final_norm=True)
    c_p, n_p, m_p = _unpack_state(c_aug_p, m_rep_p)
    token_major = lambda a: a.reshape(a.shape[0], *kv_shape, a.shape[2]).transpose(0, 3, 1, 2)

    n_new = db * dec_seq
    xs = x_sample.reshape(n_new, d)
    xs_pad = jnp.pad(x_sample, ((0, 0), (0, MLSTM_CHUNK - dec_seq), (0, 0))).reshape(db * MLSTM_CHUNK, d)
    c_aug0, m0 = _pack_state(state_c[0], state_n[0], state_m[0])
    hg_pad, c_aug_s, m_rep_s = _self_decoder(xs_pad, w, c_aug0, m0, batch=db, valid_len=dec_seq)
    hg_s = hg_pad.reshape(db, MLSTM_CHUNK, -1)[:, :dec_seq].reshape(n_new, -1)
    hs = _layer_tail(hg_s, w["w_out_a"], xs, w["g_mlp0"], w["w_up0"], w["w_dn0"], w["g_final"], final_norm=False)
    kt_s, vt_s, ktb_s, vtb_s, qb_s = _kvq(hs, w["g_kv"], w["g_mix1"], w["w_kvt"], w["w_q"],
                                          batch=1, key_block=n_new)
    new_page = lambda a: jnp.pad(a.reshape(d, db, dec_seq).transpose(1, 0, 2),
                                 ((0, 0), (0, 0), (0, page - dec_seq)))
    feature_major = lambda c: c.transpose(0, 2, 3, 1).reshape(c.shape[0], d, page)
    bias_rows = jnp.broadcast_to(jnp.tile(w["sb_bias"], dec_seq)[:, None], (dec_seq * SB_HEADS, page))
    attn_s = _sb_paged(qb_s.astype(F32).reshape(db, dec_seq, d), bias_rows, new_page(ktb_s), new_page(vtb_s),
                       feature_major(cache_k), feature_major(cache_v), page_table,
                       n_slots=min(n_pages, PAGES_PER_STEP))
    y_s = _layer_tail(attn_s.reshape(n_new, d), w["w_out_b"], hs, w["g_mlp1"], w["w_up1"], w["w_dn1"],
                      w["g_final"], final_norm=True)
    c_s, n_s, m_s = _unpack_state(c_aug_s, m_rep_s)
    new_rows = lambda a: a.reshape(*kv_shape, db, dec_seq).transpose(2, 3, 0, 1)

    return (y_p.reshape(bp, seq, d), y_s.reshape(db, dec_seq, d),
            c_p, n_p, m_p, token_major(kt_p), token_major(vt_p),
            c_s, n_s, m_s, new_rows(kt_s), new_rows(vt_s))
```

```python
import functools
import math

import jax
import jax.numpy as jnp
from jax import lax
from jax.experimental import pallas as pl
from jax.experimental.pallas import tpu as pltpu

F32 = jnp.float32
BF16 = jnp.bfloat16

EPS = 1e-6
LOG2E = 1.4426950408889634
MLSTM_HEADS = 8
MLSTM_DK = 64
MLSTM_DV = 128
MLSTM_CHUNK = 128
SB_HEADS = 16
SB_HEAD_DIM = 64

V7X_LANES = 128
V7X_VMEM_BYTES = 64 * 1024 * 1024
VMEM_LIMIT_BYTES = V7X_VMEM_BYTES - 8 * 1024 * 1024

HEADS_PER_LANE_TILE = V7X_LANES // SB_HEAD_DIM
SB_KEY_BLOCK = 256
SB_QUERY_BLOCK = 1024
SB_LOOP_UNROLL = 4
PAGES_PER_STEP = 16
NEG_INF = float("-inf")


def _compiler_params(semantics):
    return pltpu.CompilerParams(dimension_semantics=semantics, vmem_limit_bytes=VMEM_LIMIT_BYTES)


def _dot(a, b):
    return jnp.dot(a, b, preferred_element_type=F32)


def _dot_nt(a, b):
    return lax.dot_general(a, b, (((1,), (1,)), ((), ())), preferred_element_type=F32)


def _dot_exact(a, b):
    return jnp.dot(a, b, preferred_element_type=F32, precision=lax.Precision.HIGHEST)


def _rms_scale(x):
    return lax.rsqrt(jnp.mean(x * x, axis=-1, keepdims=True) + EPS)


def _softplus(z):
    return jnp.maximum(z, 0.0) + jnp.log(1.0 + jnp.exp2(jnp.abs(z) * (-LOG2E)))


def _row_tile(t, target):
    tile = min(t, target)
    assert t % tile == 0, (t, tile)
    return tile


def _inproj_kernel(x_ref, g_ref, wq_ref, wkt_ref, wv_ref, wo_ref, wg_ref,
                   q_ref, kt_ref, v_ref, o_ref, gate_ref):
    x = x_ref[...]
    xb = (x * _rms_scale(x) * g_ref[...]).astype(BF16)
    q_ref[...] = _dot(xb, wq_ref[...]).astype(BF16)
    kt_ref[...] = _dot_nt(wkt_ref[...], xb).astype(BF16)
    v_ref[...] = _dot(xb, wv_ref[...]).astype(BF16)
    o_ref[...] = _dot(xb, wo_ref[...])
    gate_ref[...] = _dot(xb, wg_ref[...])


def _inproj(x, gain, wq, wkt, wv, wo, wg):
    t, d = x.shape
    tm = _row_tile(t, 512)
    qk, vd, gw = wq.shape[1], wv.shape[1], wg.shape[1]
    full = lambda i: (0, 0)
    rows = lambda i: (i, 0)
    return pl.pallas_call(
        _inproj_kernel,
        grid=(t // tm,),
        in_specs=[
            pl.BlockSpec((tm, d), rows),
            pl.BlockSpec((1, d), full),
            pl.BlockSpec((d, qk), full),
            pl.BlockSpec((qk, d), full),
            pl.BlockSpec((d, vd), full),
            pl.BlockSpec((d, vd), full),
            pl.BlockSpec((d, gw), full),
        ],
        out_specs=[
            pl.BlockSpec((tm, qk), rows),
            pl.BlockSpec((qk, tm), lambda i: (0, i)),
            pl.BlockSpec((tm, vd), rows),
            pl.BlockSpec((tm, vd), rows),
            pl.BlockSpec((tm, gw), rows),
        ],
        out_shape=[
            jax.ShapeDtypeStruct((t, qk), BF16),
            jax.ShapeDtypeStruct((qk, t), BF16),
            jax.ShapeDtypeStruct((t, vd), BF16),
            jax.ShapeDtypeStruct((t, vd), F32),
            jax.ShapeDtypeStruct((t, gw), F32),
        ],
        compiler_params=_compiler_params(("parallel",)),
        name="mlstm_inproj",
    )(x, gain, wq, wkt, wv, wo, wg)


def _mlstm_kernel(q_ref, kt_ref, v_ref, o_ref, g_ref, bias_ref, ghead_ref, c0_ref, m0_ref,
                  hg_ref, c_ref, m_ref, rows_scr, *, valid_len):
    n_heads, dk, dv, chunk = MLSTM_HEADS, MLSTM_DK, MLSTM_DV, MLSTM_CHUNK

    @pl.when(pl.program_id(1) == 0)
    def _load_state():
        c_ref[...] = c0_ref[...]
        m_ref[...] = m0_ref[...]

    lane = lax.broadcasted_iota(jnp.int32, (chunk, V7X_LANES), 1)
    sub = lax.broadcasted_iota(jnp.int32, (chunk, V7X_LANES), 0)

    gates = g_ref[...] + bias_ref[...]
    log_f = jnp.minimum(gates, 0.0) - jnp.log1p(jnp.exp(-jnp.abs(gates)))
    if valid_len < chunk:
        valid = sub < valid_len
        log_f = jnp.where(valid, log_f, 0.0)
        gates = jnp.where(valid, gates, NEG_INF)
    is_f_lane = (lane >= n_heads) & (lane < 2 * n_heads)
    col_form = jnp.where(is_f_lane, log_f, jnp.where(lane < n_heads, gates, 0.0))
    row_form = col_form.T

    lower = (lane <= sub).astype(F32)
    upper = (sub <= lane).astype(F32)
    b_cols = _dot_exact(lower, jnp.where(is_f_lane, col_form, 0.0))
    is_f_row = (sub >= n_heads) & (sub < 2 * n_heads)
    b_rows = _dot_exact(jnp.where(is_f_row, row_form, 0.0), upper)
    rows_scr[0:chunk, :] = row_form
    rows_scr[chunk:2 * chunk, :] = b_rows

    causal = lane <= sub
    ones_block = jnp.ones((chunk, dv), BF16)
    assert chunk == V7X_LANES
    tile = lambda col: jnp.broadcast_to(col, (chunk, V7X_LANES))

    for h in range(n_heads):
        pair, half = divmod(h, HEADS_PER_LANE_TILE)
        b_col = tile(jnp.sum(jnp.where(lane == n_heads + h, b_cols, 0.0), axis=1, keepdims=True))
        ig_row = rows_scr[h:h + 1, :]
        b_row = rows_scr[chunk + n_heads + h:chunk + n_heads + h + 1, :]
        m_prev = m_ref[0, h:h + 1, :]

        log_d = jnp.where(causal, b_col - b_row + ig_row, NEG_INF)
        m_inter = b_col + m_prev
        m_t = jnp.maximum(m_inter, tile(jnp.max(log_d, axis=1, keepdims=True)))
        w_inter = jnp.exp(m_inter - m_t)
        decay_mat = jnp.exp(log_d - m_t)

        q_pair = q_ref[:, pair * V7X_LANES:(pair + 1) * V7X_LANES].astype(F32)
        q_h = jnp.where(lane // dk == half, q_pair, 0.0).astype(BF16)
        kt_pair = kt_ref[pair * V7X_LANES:(pair + 1) * V7X_LANES, :]
        s = _dot(q_h, kt_pair) * decay_mat

        v_aug = jnp.concatenate([v_ref[:, h * dv:(h + 1) * dv], ones_block], axis=1)
        c_h = c_ref[0, h]
        c_pair = jnp.concatenate([c_h, c_h], axis=0).astype(BF16)
        q_c = _dot(q_h, c_pair)
        s_v = _dot(s.astype(BF16), v_aug)
        num = w_inter * q_c[:, :dv] + s_v[:, :dv]
        den = w_inter * q_c[:, dv:] + s_v[:, dv:]
        h_val = num / jnp.maximum(jnp.abs(den), jnp.exp(-m_t))

        mean_sq = tile(jnp.mean(h_val * h_val, axis=1, keepdims=True))
        y = h_val * lax.rsqrt(mean_sq + EPS) * ghead_ref[:, h * dv:(h + 1) * dv]
        gate_o = jax.nn.sigmoid(o_ref[:, h * dv:(h + 1) * dv])
        hg_ref[:, h * dv:(h + 1) * dv] = (gate_o * y).astype(BF16)

        m_new = m_t[chunk - 1:chunk, :]
        b_last = b_col[chunk - 1:chunk, :]
        w_k = jnp.exp(b_last - b_row + ig_row - m_new)
        decay = jnp.exp(b_last + m_prev - m_new)
        kt_h = kt_pair[half * dk:(half + 1) * dk, :].astype(F32)
        c_ref[0, h] = (jnp.concatenate([decay] * (2 * dv // V7X_LANES), axis=1) * c_h
                       + _dot((kt_h * w_k).astype(BF16), v_aug))
        m_ref[0, h:h + 1, :] = m_new


def _mlstm(q, kt, v, o, gates, gate_bias, ghead, c_aug0, m0, *, batch, valid_len):
    t_total = q.shape[0]
    chunk = MLSTM_CHUNK
    n_chunks = t_total // (batch * chunk)
    qk, vd, gw = q.shape[1], v.shape[1], gates.shape[1]
    rows = lambda b, c: (b * n_chunks + c, 0)
    full = lambda b, c: (0, 0)
    state4 = lambda b, c: (b, 0, 0, 0)
    state3 = lambda b, c: (b, 0, 0)
    return pl.pallas_call(
        functools.partial(_mlstm_kernel, valid_len=valid_len),
        grid=(batch, n_chunks),
        in_specs=[
            pl.BlockSpec((chunk, qk), rows),
            pl.BlockSpec((qk, chunk), lambda b, c: (0, b * n_chunks + c)),
            pl.BlockSpec((chunk, vd), rows),
            pl.BlockSpec((chunk, vd), rows),
            pl.BlockSpec((chunk, gw), rows),
            pl.BlockSpec((1, gw), full),
            pl.BlockSpec((1, vd), full),
            pl.BlockSpec((1,) + c_aug0.shape[1:], state4),
            pl.BlockSpec((1,) + m0.shape[1:], state3),
        ],
        out_specs=[
            pl.BlockSpec((chunk, vd), rows),
            pl.BlockSpec((1,) + c_aug0.shape[1:], state4),
            pl.BlockSpec((1,) + m0.shape[1:], state3),
        ],
        out_shape=[
            jax.ShapeDtypeStruct((t_total, vd), BF16),
            jax.ShapeDtypeStruct(c_aug0.shape, F32),
            jax.ShapeDtypeStruct(m0.shape, F32),
        ],
        scratch_shapes=[pltpu.VMEM((2 * chunk, V7X_LANES), F32)],
        compiler_params=_compiler_params(("parallel", "arbitrary")),
        name="mlstm_chunk",
    )(q, kt, v, o, gates, gate_bias, ghead, c_aug0, m0)


def _layer_tail_kernel(mix_ref, wout_ref, res_ref, g_ref, wup_ref, wdn_ref, gfin_ref, out_ref,
                       hn_scr, acc_scr, *, final_norm):
    j = pl.program_id(1)

    @pl.when(j == 0)
    def _start():
        h = res_ref[...] + _dot(mix_ref[...], wout_ref[...])
        hn_scr[...] = (h * _rms_scale(h) * g_ref[...]).astype(BF16)
        acc_scr[...] = h

    u = _dot(hn_scr[...], wup_ref[...])
    act = jnp.square(jnp.maximum(u, 0.0)).astype(BF16)
    acc_scr[...] += _dot(act, wdn_ref[...])

    @pl.when(j == pl.num_programs(1) - 1)
    def _finish():
        y = acc_scr[...]
        if final_norm:
            y = y * _rms_scale(y) * gfin_ref[...]
        out_ref[...] = y


def _layer_tail(mix, w_out, res, gain, w_up, w_down, gain_final, *, final_norm):
    t, d = res.shape
    k = mix.shape[1]
    ff = w_up.shape[1]
    tm = _row_tile(t, 1024)
    tf = _row_tile(ff, 1024)
    rows = lambda i, j: (i, 0)
    full = lambda i, j: (0, 0)
    return pl.pallas_call(
        functools.partial(_layer_tail_kernel, final_norm=final_norm),
        grid=(t // tm, ff // tf),
        in_specs=[
            pl.BlockSpec((tm, k), rows),
            pl.BlockSpec((k, d), full),
            pl.BlockSpec((tm, d), rows),
            pl.BlockSpec((1, d), full),
            pl.BlockSpec((d, tf), lambda i, j: (0, j)),
            pl.BlockSpec((tf, d), lambda i, j: (j, 0)),
            pl.BlockSpec((1, d), full),
        ],
        out_specs=pl.BlockSpec((tm, d), rows),
        out_shape=jax.ShapeDtypeStruct((t, d), F32),
        scratch_shapes=[pltpu.VMEM((tm, d), BF16), pltpu.VMEM((tm, d), F32)],
        compiler_params=_compiler_params(("parallel", "arbitrary")),
        name="layer_tail",
    )(mix, w_out, res, gain, w_up, w_down, gain_final)


def _kvq_kernel(h_ref, gkv_ref, gq_ref, wkvt_ref, wq_ref, kt_ref, vt_ref, ktb_ref, vtb_ref, qb_ref):
    h = h_ref[...]
    hn = h * _rms_scale(h)
    kvt = _dot_nt(wkvt_ref[...], (hn * gkv_ref[...]).astype(BF16))
    d = kt_ref.shape[1]
    kt, vt = kvt[:d], kvt[d:]
    kt_ref[0] = kt
    vt_ref[0] = vt
    key_block = ktb_ref.shape[3]
    for blk in range(ktb_ref.shape[1]):
        cols = slice(blk * key_block, (blk + 1) * key_block)
        ktb_ref[0, blk] = kt[:, cols].astype(BF16)
        vtb_ref[0, blk] = vt[:, cols].astype(BF16)
    qb_ref[...] = _dot((hn * gq_ref[...]).astype(BF16), wq_ref[...]).astype(BF16)


def _kvq(h, gain_kv, gain_q, w_kvt, w_q, *, batch, key_block):
    t_total, d = h.shape
    t = t_total // batch
    tm = _row_tile(t, 512)
    assert tm % key_block == 0, (tm, key_block)
    nt = t // tm
    n = w_q.shape[1]
    rows = lambda b, i: (b * nt + i, 0)
    full = lambda b, i: (0, 0)
    feat = pl.BlockSpec((1, n, tm), lambda b, i: (b, 0, i))
    blocks = pl.BlockSpec((1, tm // key_block, n, key_block), lambda b, i: (b, i, 0, 0))
    return pl.pallas_call(
        _kvq_kernel,
        grid=(batch, nt),
        in_specs=[pl.BlockSpec((tm, d), rows), pl.BlockSpec((1, d), full), pl.BlockSpec((1, d), full),
                  pl.BlockSpec((2 * n, d), full), pl.BlockSpec((d, n), full)],
        out_specs=[feat, feat, blocks, blocks, pl.BlockSpec((tm, n), rows)],
        out_shape=[jax.ShapeDtypeStruct((batch, n, t), F32), jax.ShapeDtypeStruct((batch, n, t), F32),
                   jax.ShapeDtypeStruct((batch, t // key_block, n, key_block), BF16),
                   jax.ShapeDtypeStruct((batch, t // key_block, n, key_block), BF16),
                   jax.ShapeDtypeStruct((t_total, n), BF16)],
        compiler_params=_compiler_params(("parallel", "parallel")),
        name="kvq_proj",
    )(h, gain_kv, gain_q, w_kvt, w_q)


def _suffix_matrix(tk, with_total):
    n = tk + (V7X_LANES if with_total else 0)
    j = lax.broadcasted_iota(jnp.int32, (tk, n), 0)
    s = lax.broadcasted_iota(jnp.int32, (tk, n), 1)
    return ((j >= s) | (s >= tk)).astype(BF16)


def _sb_prompt_kernel(bias_ref, q_ref, kt_ref, vt_ref, suffix_ref, out_ref,
                      logw_scr, blocksum_scr, total_scr, acc_scr, *, tq, tk):
    pair = pl.program_id(1)
    i = pl.program_id(2)
    n_rep = tk // V7X_LANES
    heads = range(HEADS_PER_LANE_TILE)
    lane = lax.broadcasted_iota(jnp.int32, (tq, V7X_LANES), 1)
    q_pair = q_ref[...].astype(F32)
    q_heads = [jnp.where(lane // SB_HEAD_DIM == e, q_pair, 0.0).astype(BF16) for e in heads]
    biases = [bias_ref[pair * HEADS_PER_LANE_TILE + e] for e in heads]
    suffix = suffix_ref[...]

    def score_rows(j, slot, r0, r1, diagonal):
        kt = kt_ref[0, j]
        if diagonal:
            row = lax.broadcasted_iota(jnp.int32, (r1 - r0, tk), 0)
            col = lax.broadcasted_iota(jnp.int32, (r1 - r0, tk), 1)
            visible = col < row
        for e in heads:
            z = _dot(q_heads[e][r0:r1], kt) + biases[e]
            sp = _softplus(z)
            if diagonal:
                sp = jnp.where(visible, sp, 0.0)
            log_w = z - _dot(sp.astype(BF16), suffix)
            if diagonal:
                log_w = jnp.where(visible, log_w, NEG_INF)
            logw_scr[slot, e, r0:r1, :] = log_w
            blocksum_scr[slot, e, r0:r1, :] = jnp.broadcast_to(jnp.sum(sp, axis=1, keepdims=True),
                                                               (r1 - r0, V7X_LANES))

    def weigh_rows(j, slot, r0):
        vt = vt_ref[0, j]
        for e in heads:
            total = total_scr[e, r0:, :]
            a = jnp.exp(logw_scr[slot, e, r0:, :] - jnp.concatenate([total] * n_rep, axis=1))
            acc_scr[e, r0:, :] += _dot_nt(a.astype(BF16), vt)
            total_scr[e, r0:, :] = total + blocksum_scr[slot, e, r0:, :]

    total_scr[...] = jnp.zeros_like(total_scr)
    acc_scr[...] = jnp.zeros_like(acc_scr)
    ratio = tq // tk
    base = i * ratio
    n_slots = logw_scr.shape[0]
    assert ratio % n_slots == 0

    def score_diagonal(d):
        score_rows(base + d, d % n_slots, d * tk, (d + 1) * tk, True)
        if (d + 1) * tk < tq:
            score_rows(base + d, d % n_slots, (d + 1) * tk, tq, False)

    score_diagonal(ratio - 1)
    for d in range(ratio - 1, 0, -1):
        weigh_rows(base + d, d % n_slots, d * tk)
        score_diagonal(d - 1)

    def body(it, carry):
        j = base - it * n_slots
        for u in range(n_slots):
            weigh_rows(j - u, (-u) % n_slots, 0)
            score_rows(j - u - 1, (-u - 1) % n_slots, 0, tq, False)
        return carry

    lax.fori_loop(0, base // n_slots, body, 0)
    weigh_rows(0, 0, 0)
    out = acc_scr[0]
    for e in heads[1:]:
        out = jnp.where(lane // SB_HEAD_DIM == e, acc_scr[e], out)
    out_ref[...] = out.astype(BF16)


def _sb_prompt(q, kt_blocks, vt_blocks, bias, *, batch):
    t_total, d = q.shape
    t = t_total // batch
    n_blocks, tk = kt_blocks.shape[1], kt_blocks.shape[3]
    tq = min(t, SB_QUERY_BLOCK)
    assert tq % tk == 0 and t % tq == 0, (t, tq, tk)
    nq = t // tq
    n_pairs = d // V7X_LANES
    kv_spec = pl.BlockSpec((1, n_blocks, V7X_LANES, tk), lambda b, p, i: (b, 0, p, 0))
    head_tiles = lambda lanes: pltpu.VMEM((HEADS_PER_LANE_TILE, tq, lanes), F32)
    n_slots = math.gcd(tq // tk, SB_LOOP_UNROLL)
    slot_tiles = lambda lanes: pltpu.VMEM((n_slots, HEADS_PER_LANE_TILE, tq, lanes), F32)
    return pl.pallas_call(
        functools.partial(_sb_prompt_kernel, tq=tq, tk=tk),
        grid=(batch, n_pairs, nq),
        in_specs=[
            pl.BlockSpec(memory_space=pltpu.SMEM),
            pl.BlockSpec((tq, V7X_LANES), lambda b, p, i: (b * nq + i, p)),
            kv_spec,
            kv_spec,
            pl.BlockSpec((tk, tk), lambda b, p, i: (0, 0)),
        ],
        out_specs=pl.BlockSpec((tq, V7X_LANES), lambda b, p, i: (b * nq + i, p)),
        out_shape=jax.ShapeDtypeStruct((t_total, d), BF16),
        scratch_shapes=[slot_tiles(tk), slot_tiles(V7X_LANES), head_tiles(V7X_LANES), head_tiles(V7X_LANES)],
        compiler_params=_compiler_params(("parallel", "parallel", "arbitrary")),
        name="sb_attention_prompt",
    )(bias, q, kt_blocks, vt_blocks, _suffix_matrix(tk, with_total=False))


def _sb_paged_kernel(pt_ref, q_ref, bias_ref, knew_ref, vnew_ref, *rest, n_q, page, n_slots):
    del pt_ref
    k_refs, v_refs = rest[:n_slots], rest[n_slots:2 * n_slots]
    out_ref, qbd_scr, total_scr, acc_scr = rest[2 * n_slots:]
    step = pl.program_id(1)
    n_rows = n_q * SB_HEADS
    d = q_ref.shape[2]
    n_rep = page // V7X_LANES
    row_d = lax.broadcasted_iota(jnp.int32, (SB_HEADS, d), 0)
    col_d = lax.broadcasted_iota(jnp.int32, (SB_HEADS, d), 1)
    head_cols = col_d // SB_HEAD_DIM == row_d

    j = lax.broadcasted_iota(jnp.int32, (page, page + V7X_LANES), 0)
    s = lax.broadcasted_iota(jnp.int32, (page, page + V7X_LANES), 1)
    suffix = ((j >= s) | (s >= page)).astype(BF16)

    def visit(pages, visible):
        logits = [_dot(qbd_scr[...], kt_page) + bias_ref[...] for kt_page, _ in pages]
        softplus = [_softplus(z) for z in logits]
        if visible is not None:
            softplus = [jnp.where(visible, sp, 0.0) for sp in softplus]
        sums = [_dot(sp.astype(BF16), suffix) for sp in softplus]
        log_ws = [z - s[:, :page] for z, s in zip(logits, sums)]
        if visible is not None:
            log_ws = [jnp.where(visible, lw, NEG_INF) for lw in log_ws]
        total = total_scr[...]
        weights = []
        for log_w, s in zip(log_ws, sums):
            weights.append(jnp.exp(log_w - jnp.concatenate([total] * n_rep, axis=1)).astype(BF16))
            total = total + s[:, page:]
        acc = acc_scr[...]
        for a, (_, vt_page) in zip(weights, pages):
            acc = acc + _dot_nt(a, vt_page)
        total_scr[...] = total
        acc_scr[...] = acc

    @pl.when(step == 0)
    def _new_tokens():
        q = q_ref[0]
        blocks = [jnp.where(head_cols, jnp.broadcast_to(q[i:i + 1, :], (SB_HEADS, d)), 0.0)
                  for i in range(n_q)]
        qbd_scr[...] = jnp.concatenate(blocks, axis=0).astype(BF16)
        total_scr[...] = jnp.zeros_like(total_scr)
        acc_scr[...] = jnp.zeros_like(acc_scr)
        q_idx = lax.broadcasted_iota(jnp.int32, (n_rows, page), 0) // SB_HEADS
        key_idx = lax.broadcasted_iota(jnp.int32, (n_rows, page), 1)
        visit([(knew_ref[0], vnew_ref[0])], key_idx < q_idx)

    @pl.when(step > 0)
    def _past_pages():
        visit([(k_refs[p][0].astype(BF16), v_refs[p][0].astype(BF16)) for p in range(n_slots)], None)

    @pl.when(step == pl.num_programs(1) - 1)
    def _emit():
        acc = acc_scr[...]
        rows = []
        for i in range(n_q):
            blk = jnp.where(head_cols, acc[i * SB_HEADS:(i + 1) * SB_HEADS, :], 0.0)
            rows.append(jnp.sum(blk, axis=0, keepdims=True))
        out_ref[0] = jnp.concatenate(rows, axis=0).astype(BF16)


def _sb_paged(q, bias_rows, kt_new_pad, vt_new_pad, cache_kt, cache_vt, page_table, *, n_slots):
    batch, n_q, d = q.shape
    n_pages = page_table.shape[1]
    page = cache_kt.shape[2]
    n_rows = n_q * SB_HEADS
    assert n_pages % n_slots == 0, (n_pages, n_slots)

    def page_spec(slot):
        def index(b, s, pt):
            logical = n_pages - 1 - ((jnp.maximum(s, 1) - 1) * n_slots + slot)
            return (pt[b, logical], 0, 0)
        return pl.BlockSpec((1, d, page), index)

    per_seq = lambda b, s, pt: (b, 0, 0)
    grid_spec = pltpu.PrefetchScalarGridSpec(
        num_scalar_prefetch=1,
        grid=(batch, n_pages // n_slots + 1),
        in_specs=[
            pl.BlockSpec((1, n_q, d), per_seq),
            pl.BlockSpec(bias_rows.shape, lambda b, s, pt: (0, 0)),
            pl.BlockSpec((1, d, page), per_seq),
            pl.BlockSpec((1, d, page), per_seq),
        ] + [page_spec(p) for p in range(n_slots)] * 2,
        out_specs=pl.BlockSpec((1, n_q, d), per_seq),
        scratch_shapes=[pltpu.VMEM((n_rows, d), BF16), pltpu.VMEM((n_rows, V7X_LANES), F32),
                        pltpu.VMEM((n_rows, d), F32)],
    )
    return pl.pallas_call(
        functools.partial(_sb_paged_kernel, n_q=n_q, page=page, n_slots=n_slots),
        grid_spec=grid_spec,
        out_shape=jax.ShapeDtypeStruct((batch, n_q, d), BF16),
        compiler_params=_compiler_params(("parallel", "arbitrary")),
        name="sb_attention_paged",
    )(page_table, q, bias_rows, kt_new_pad, vt_new_pad, *([cache_kt] * n_slots), *([cache_vt] * n_slots))


def _prep_weights(norm_mix, norm_mlp, w_in_a, b_i, b_f, head_norm_a, w_out_a, norm_kv, w_kv, w_q_b,
                  w_out_b, sb_bias, w_up, w_down, norm_final):
    qk = MLSTM_HEADS * MLSTM_DK
    vd = MLSTM_HEADS * MLSTM_DV
    w_in = w_in_a[0]
    d = w_in.shape[0]
    gate_w = jnp.zeros((d, V7X_LANES), F32).at[:, :2 * MLSTM_HEADS].set(w_in[:, 2 * qk + 2 * vd:])
    gate_b = jnp.zeros((1, V7X_LANES), F32).at[0, :MLSTM_HEADS].set(b_i[0])
    gate_b = gate_b.at[0, MLSTM_HEADS:2 * MLSTM_HEADS].set(b_f[0])
    row = lambda g: g.reshape(1, -1).astype(F32)
    return dict(
        g_mix0=row(norm_mix[0]), g_mix1=row(norm_mix[1]), g_mlp0=row(norm_mlp[0]), g_mlp1=row(norm_mlp[1]),
        g_kv=row(norm_kv), g_final=row(norm_final), g_head=row(head_norm_a[0]),
        wq=w_in[:, :qk].astype(BF16),
        wkt=(w_in[:, qk:2 * qk] * (MLSTM_DK ** -0.5)).T.astype(BF16),
        wv=w_in[:, 2 * qk:2 * qk + vd].astype(BF16),
        wo=w_in[:, 2 * qk + vd:2 * qk + 2 * vd].astype(BF16),
        wg=gate_w.astype(BF16), gate_b=gate_b,
        w_out_a=w_out_a[0].astype(BF16),
        w_kvt=w_kv.T.astype(BF16),
        w_q=(w_q_b[0] * (SB_HEAD_DIM ** -0.5)).astype(BF16),
        w_out_b=w_out_b[0].astype(BF16),
        sb_bias=sb_bias[0].astype(F32),
        w_up0=w_up[0].astype(BF16), w_up1=w_up[1].astype(BF16),
        w_dn0=w_down[0].astype(BF16), w_dn1=w_down[1].astype(BF16),
    )


def _pack_state(c, n, m):
    n_rep = jnp.broadcast_to(n.astype(F32)[..., None], n.shape + (MLSTM_DV,))
    c_aug = jnp.concatenate([c.astype(F32), n_rep], axis=-1)
    return c_aug, jnp.broadcast_to(m.astype(F32)[..., None], m.shape + (V7X_LANES,))


def _unpack_state(c_aug, m_rep):
    return c_aug[..., :MLSTM_DV][None], c_aug[..., MLSTM_DV][None], m_rep[..., 0][None]


def _self_decoder(x2d, w, c_aug0, m0, *, batch, valid_len):
    q, kt, v, o, gates = _inproj(x2d, w["g_mix0"], w["wq"], w["wkt"], w["wv"], w["wo"], w["wg"])
    hg, c_aug, m_rep = _mlstm(q, kt, v, o, gates, w["gate_b"], w["g_head"], c_aug0, m0,
                              batch=batch, valid_len=valid_len)
    return hg, c_aug, m_rep


def kernel(x_prompt, x_sample, state_c, state_n, state_m, cache_k, cache_v, page_table, norm_mix, norm_mlp,
           w_in_a, b_i, b_f, head_norm_a, w_out_a, norm_kv, w_kv, w_q_b, w_out_b, sb_bias, w_up, w_down,
           norm_final):
    assert w_in_a.shape[0] == 1 and w_q_b.shape[0] == 1, "one self-decoder and one cross-decoder layer"
    w = _prep_weights(norm_mix, norm_mlp, w_in_a, b_i, b_f, head_norm_a, w_out_a, norm_kv, w_kv, w_q_b,
                      w_out_b, sb_bias, w_up, w_down, norm_final)
    bp, seq, d = x_prompt.shape
    db, dec_seq, _ = x_sample.shape
    n_pages, page = page_table.shape[1], cache_k.shape[1]
    kv_shape = (SB_HEADS, SB_HEAD_DIM)

    xp = x_prompt.reshape(bp * seq, d)
    zeros_c = jnp.zeros((bp, MLSTM_HEADS, MLSTM_DK, 2 * MLSTM_DV), F32)
    zeros_m = jnp.zeros((bp, MLSTM_HEADS, V7X_LANES), F32)
    hg, c_aug_p, m_rep_p = _self_decoder(xp, w, zeros_c, zeros_m, batch=bp, valid_len=MLSTM_CHUNK)
    h = _layer_tail(hg, w["w_out_a"], xp, w["g_mlp0"], w["w_up0"], w["w_dn0"], w["g_final"], final_norm=False)
    kt_p, vt_p, ktb, vtb, qb = _kvq(h, w["g_kv"], w["g_mix1"], w["w_kvt"], w["w_q"],
                                    batch=bp, key_block=min(seq, SB_KEY_BLOCK))
    attn = _sb_prompt(qb, ktb, vtb, w["sb_bias"], batch=bp)
    y_p = _layer_tail(attn, w["w_out_b"], h, w["g_mlp1"], w["w_up1"], w["w_dn1"], w["g_final"], final_norm=True)
    c_p, n_p, m_p = _unpack_state(c_aug_p, m_rep_p)
    token_major = lambda a: a.reshape(a.shape[0], *kv_shape, a.shape[2]).transpose(0, 3, 1, 2)

    n_new = db * dec_seq
    xs = x_sample.reshape(n_new, d)
    xs_pad = jnp.pad(x_sample, ((0, 0), (0, MLSTM_CHUNK - dec_seq), (0, 0))).reshape(db * MLSTM_CHUNK, d)
    c_aug0, m0 = _pack_state(state_c[0], state_n[0], state_m[0])
    hg_pad, c_aug_s, m_rep_s = _self_decoder(xs_pad, w, c_aug0, m0, batch=db, valid_len=dec_seq)
    hg_s = hg_pad.reshape(db, MLSTM_CHUNK, -1)[:, :dec_seq].reshape(n_new, -1)
    hs = _layer_tail(hg_s, w["w_out_a"], xs, w["g_mlp0"], w["w_up0"], w["w_dn0"], w["g_final"], final_norm=False)
    kt_s, vt_s, ktb_s, vtb_s, qb_s = _kvq(hs, w["g_kv"], w["g_mix1"], w["w_kvt"], w["w_q"],
                                          batch=1, key_block=n_new)
    new_page = lambda a: jnp.pad(a.reshape(d, db, dec_seq).transpose(1, 0, 2),
                                 ((0, 0), (0, 0), (0, page - dec_seq)))
    feature_major = lambda c: c.transpose(0, 2, 3, 1).reshape(c.shape[0], d, page)
    bias_rows = jnp.broadcast_to(jnp.tile(w["sb_bias"], dec_seq)[:, None], (dec_seq * SB_HEADS, page))
    attn_s = _sb_paged(qb_s.astype(F32).reshape(db, dec_seq, d), bias_rows, new_page(ktb_s), new_page(vtb_s),
                       feature_major(cache_k), feature_major(cache_v), page_table,
                       n_slots=min(n_pages, PAGES_PER_STEP))
    y_s = _layer_tail(attn_s.reshape(n_new, d), w["w_out_b"], hs, w["g_mlp1"], w["w_up1"], w["w_dn1"],
                      w["g_final"], final_norm=True)
    c_s, n_s, m_s = _unpack_state(c_aug_s, m_rep_s)
    new_rows = lambda a: a.reshape(*kv_shape, db, dec_seq).transpose(2, 3, 0, 1)

    return (y_p.reshape(bp, seq, d), y_s.reshape(db, dec_seq, d),
            c_p, n_p, m_p, token_major(kt_p), token_major(vt_p),
            c_s, n_s, m_s, new_rows(kt_s), new_rows(vt_s))
```

```python
import functools
import math

import jax
import jax.numpy as jnp
from jax import lax
from jax.experimental import pallas as pl
from jax.experimental.pallas import tpu as pltpu

F32 = jnp.float32
BF16 = jnp.bfloat16

EPS = 1e-6
LOG2E = 1.4426950408889634
MLSTM_HEADS = 8
MLSTM_DK = 64
MLSTM_DV = 128
MLSTM_CHUNK = 128
SB_HEADS = 16
SB_HEAD_DIM = 64

V7X_LANES = 128
V7X_VMEM_BYTES = 64 * 1024 * 1024
VMEM_LIMIT_BYTES = V7X_VMEM_BYTES - 8 * 1024 * 1024

HEADS_PER_LANE_TILE = V7X_LANES // SB_HEAD_DIM
SB_KEY_BLOCK = 256
SB_QUERY_BLOCK = 1024
SB_LOOP_UNROLL = 4
NEG_INF = float("-inf")


def _compiler_params(semantics):
    return pltpu.CompilerParams(dimension_semantics=semantics, vmem_limit_bytes=VMEM_LIMIT_BYTES)


def _dot(a, b):
    return jnp.dot(a, b, preferred_element_type=F32)


def _dot_nt(a, b):
    return lax.dot_general(a, b, (((1,), (1,)), ((), ())), preferred_element_type=F32)


def _dot_exact(a, b):
    return jnp.dot(a, b, preferred_element_type=F32, precision=lax.Precision.HIGHEST)


def _rms_scale(x):
    return lax.rsqrt(jnp.mean(x * x, axis=-1, keepdims=True) + EPS)


def _softplus(z):
    return jnp.maximum(z, 0.0) + jnp.log(1.0 + jnp.exp2(jnp.abs(z) * (-LOG2E)))


def _row_tile(t, target):
    tile = min(t, target)
    assert t % tile == 0, (t, tile)
    return tile


def _inproj_kernel(x_ref, g_ref, wq_ref, wkt_ref, wv_ref, wo_ref, wg_ref,
                   q_ref, kt_ref, v_ref, o_ref, gate_ref):
    x = x_ref[...]
    xb = (x * _rms_scale(x) * g_ref[...]).astype(BF16)
    q_ref[...] = _dot(xb, wq_ref[...]).astype(BF16)
    kt_ref[...] = _dot_nt(wkt_ref[...], xb).astype(BF16)
    v_ref[...] = _dot(xb, wv_ref[...]).astype(BF16)
    o_ref[...] = _dot(xb, wo_ref[...])
    gate_ref[...] = _dot(xb, wg_ref[...])


def _inproj(x, gain, wq, wkt, wv, wo, wg):
    t, d = x.shape
    tm = _row_tile(t, 512)
    qk, vd, gw = wq.shape[1], wv.shape[1], wg.shape[1]
    full = lambda i: (0, 0)
    rows = lambda i: (i, 0)
    return pl.pallas_call(
        _inproj_kernel,
        grid=(t // tm,),
        in_specs=[
            pl.BlockSpec((tm, d), rows),
            pl.BlockSpec((1, d), full),
            pl.BlockSpec((d, qk), full),
            pl.BlockSpec((qk, d), full),
            pl.BlockSpec((d, vd), full),
            pl.BlockSpec((d, vd), full),
            pl.BlockSpec((d, gw), full),
        ],
        out_specs=[
            pl.BlockSpec((tm, qk), rows),
            pl.BlockSpec((qk, tm), lambda i: (0, i)),
            pl.BlockSpec((tm, vd), rows),
            pl.BlockSpec((tm, vd), rows),
            pl.BlockSpec((tm, gw), rows),
        ],
        out_shape=[
            jax.ShapeDtypeStruct((t, qk), BF16),
            jax.ShapeDtypeStruct((qk, t), BF16),
            jax.ShapeDtypeStruct((t, vd), BF16),
            jax.ShapeDtypeStruct((t, vd), F32),
            jax.ShapeDtypeStruct((t, gw), F32),
        ],
        compiler_params=_compiler_params(("parallel",)),
        name="mlstm_inproj",
    )(x, gain, wq, wkt, wv, wo, wg)


def _mlstm_kernel(q_ref, kt_ref, v_ref, o_ref, g_ref, bias_ref, ghead_ref, c0_ref, m0_ref,
                  hg_ref, c_ref, m_ref, rows_scr, *, valid_len):
    n_heads, dk, dv, chunk = MLSTM_HEADS, MLSTM_DK, MLSTM_DV, MLSTM_CHUNK

    @pl.when(pl.program_id(1) == 0)
    def _load_state():
        c_ref[...] = c0_ref[...]
        m_ref[...] = m0_ref[...]

    lane = lax.broadcasted_iota(jnp.int32, (chunk, V7X_LANES), 1)
    sub = lax.broadcasted_iota(jnp.int32, (chunk, V7X_LANES), 0)

    gates = g_ref[...] + bias_ref[...]
    log_f = jnp.minimum(gates, 0.0) - jnp.log1p(jnp.exp(-jnp.abs(gates)))
    if valid_len < chunk:
        valid = sub < valid_len
        log_f = jnp.where(valid, log_f, 0.0)
        gates = jnp.where(valid, gates, NEG_INF)
    is_f_lane = (lane >= n_heads) & (lane < 2 * n_heads)
    col_form = jnp.where(is_f_lane, log_f, jnp.where(lane < n_heads, gates, 0.0))
    row_form = col_form.T

    lower = (lane <= sub).astype(F32)
    upper = (sub <= lane).astype(F32)
    b_cols = _dot_exact(lower, jnp.where(is_f_lane, col_form, 0.0))
    is_f_row = (sub >= n_heads) & (sub < 2 * n_heads)
    b_rows = _dot_exact(jnp.where(is_f_row, row_form, 0.0), upper)
    rows_scr[0:chunk, :] = row_form
    rows_scr[chunk:2 * chunk, :] = b_rows

    causal = lane <= sub
    ones_block = jnp.ones((chunk, dv), BF16)
    assert chunk == V7X_LANES
    tile = lambda col: jnp.broadcast_to(col, (chunk, V7X_LANES))

    for h in range(n_heads):
        pair, half = divmod(h, HEADS_PER_LANE_TILE)
        b_col = tile(jnp.sum(jnp.where(lane == n_heads + h, b_cols, 0.0), axis=1, keepdims=True))
        ig_row = rows_scr[h:h + 1, :]
        b_row = rows_scr[chunk + n_heads + h:chunk + n_heads + h + 1, :]
        m_prev = m_ref[0, h:h + 1, :]

        log_d = jnp.where(causal, b_col - b_row + ig_row, NEG_INF)
        m_inter = b_col + m_prev
        m_t = jnp.maximum(m_inter, tile(jnp.max(log_d, axis=1, keepdims=True)))
        w_inter = jnp.exp(m_inter - m_t)
        decay_mat = jnp.exp(log_d - m_t)

        q_pair = q_ref[:, pair * V7X_LANES:(pair + 1) * V7X_LANES].astype(F32)
        q_h = jnp.where(lane // dk == half, q_pair, 0.0).astype(BF16)
        kt_pair = kt_ref[pair * V7X_LANES:(pair + 1) * V7X_LANES, :]
        s = _dot(q_h, kt_pair) * decay_mat

        v_aug = jnp.concatenate([v_ref[:, h * dv:(h + 1) * dv], ones_block], axis=1)
        c_h = c_ref[0, h]
        c_pair = jnp.concatenate([c_h, c_h], axis=0).astype(BF16)
        q_c = _dot(q_h, c_pair)
        s_v = _dot(s.astype(BF16), v_aug)
        num = w_inter * q_c[:, :dv] + s_v[:, :dv]
        den = w_inter * q_c[:, dv:] + s_v[:, dv:]
        h_val = num / jnp.maximum(jnp.abs(den), jnp.exp(-m_t))

        mean_sq = tile(jnp.mean(h_val * h_val, axis=1, keepdims=True))
        y = h_val * lax.rsqrt(mean_sq + EPS) * ghead_ref[:, h * dv:(h + 1) * dv]
        gate_o = jax.nn.sigmoid(o_ref[:, h * dv:(h + 1) * dv])
        hg_ref[:, h * dv:(h + 1) * dv] = (gate_o * y).astype(BF16)

        m_new = m_t[chunk - 1:chunk, :]
        b_last = b_col[chunk - 1:chunk, :]
        w_k = jnp.exp(b_last - b_row + ig_row - m_new)
        decay = jnp.exp(b_last + m_prev - m_new)
        kt_h = kt_pair[half * dk:(half + 1) * dk, :].astype(F32)
        c_ref[0, h] = (jnp.concatenate([decay] * (2 * dv // V7X_LANES), axis=1) * c_h
                       + _dot((kt_h * w_k).astype(BF16), v_aug))
        m_ref[0, h:h + 1, :] = m_new


def _mlstm(q, kt, v, o, gates, gate_bias, ghead, c_aug0, m0, *, batch, valid_len):
    t_total = q.shape[0]
    chunk = MLSTM_CHUNK
    n_chunks = t_total // (batch * chunk)
    qk, vd, gw = q.shape[1], v.shape[1], gates.shape[1]
    rows = lambda b, c: (b * n_chunks + c, 0)
    full = lambda b, c: (0, 0)
    state4 = lambda b, c: (b, 0, 0, 0)
    state3 = lambda b, c: (b, 0, 0)
    return pl.pallas_call(
        functools.partial(_mlstm_kernel, valid_len=valid_len),
        grid=(batch, n_chunks),
        in_specs=[
            pl.BlockSpec((chunk, qk), rows),
            pl.BlockSpec((qk, chunk), lambda b, c: (0, b * n_chunks + c)),
            pl.BlockSpec((chunk, vd), rows),
            pl.BlockSpec((chunk, vd), rows),
            pl.BlockSpec((chunk, gw), rows),
            pl.BlockSpec((1, gw), full),
            pl.BlockSpec((1, vd), full),
            pl.BlockSpec((1,) + c_aug0.shape[1:], state4),
            pl.BlockSpec((1,) + m0.shape[1:], state3),
        ],
        out_specs=[
            pl.BlockSpec((chunk, vd), rows),
            pl.BlockSpec((1,) + c_aug0.shape[1:], state4),
            pl.BlockSpec((1,) + m0.shape[1:], state3),
        ],
        out_shape=[
            jax.ShapeDtypeStruct((t_total, vd), BF16),
            jax.ShapeDtypeStruct(c_aug0.shape, F32),
            jax.ShapeDtypeStruct(m0.shape, F32),
        ],
        scratch_shapes=[pltpu.VMEM((2 * chunk, V7X_LANES), F32)],
        compiler_params=_compiler_params(("parallel", "arbitrary")),
        name="mlstm_chunk",
    )(q, kt, v, o, gates, gate_bias, ghead, c_aug0, m0)


def _layer_tail_kernel(mix_ref, wout_ref, res_ref, g_ref, wup_ref, wdn_ref, gfin_ref, out_ref,
                       hn_scr, acc_scr, *, final_norm):
    j = pl.program_id(1)

    @pl.when(j == 0)
    def _start():
        h = res_ref[...] + _dot(mix_ref[...], wout_ref[...])
        hn_scr[...] = (h * _rms_scale(h) * g_ref[...]).astype(BF16)
        acc_scr[...] = h

    u = _dot(hn_scr[...], wup_ref[...])
    act = jnp.square(jnp.maximum(u, 0.0)).astype(BF16)
    acc_scr[...] += _dot(act, wdn_ref[...])

    @pl.when(j == pl.num_programs(1) - 1)
    def _finish():
        y = acc_scr[...]
        if final_norm:
            y = y * _rms_scale(y) * gfin_ref[...]
        out_ref[...] = y


def _layer_tail(mix, w_out, res, gain, w_up, w_down, gain_final, *, final_norm):
    t, d = res.shape
    k = mix.shape[1]
    ff = w_up.shape[1]
    tm = _row_tile(t, 1024)
    tf = _row_tile(ff, 1024)
    rows = lambda i, j: (i, 0)
    full = lambda i, j: (0, 0)
    return pl.pallas_call(
        functools.partial(_layer_tail_kernel, final_norm=final_norm),
        grid=(t // tm, ff // tf),
        in_specs=[
            pl.BlockSpec((tm, k), rows),
            pl.BlockSpec((k, d), full),
            pl.BlockSpec((tm, d), rows),
            pl.BlockSpec((1, d), full),
            pl.BlockSpec((d, tf), lambda i, j: (0, j)),
            pl.BlockSpec((tf, d), lambda i, j: (j, 0)),
            pl.BlockSpec((1, d), full),
        ],
        out_specs=pl.BlockSpec((tm, d), rows),
        out_shape=jax.ShapeDtypeStruct((t, d), F32),
        scratch_shapes=[pltpu.VMEM((tm, d), BF16), pltpu.VMEM((tm, d), F32)],
        compiler_params=_compiler_params(("parallel", "arbitrary")),
        name="layer_tail",
    )(mix, w_out, res, gain, w_up, w_down, gain_final)


def _kvq_kernel(h_ref, gkv_ref, gq_ref, wkvt_ref, wq_ref, kt_ref, vt_ref, ktb_ref, vtb_ref, qb_ref):
    h = h_ref[...]
    hn = h * _rms_scale(h)
    kvt = _dot_nt(wkvt_ref[...], (hn * gkv_ref[...]).astype(BF16))
    d = kt_ref.shape[1]
    kt, vt = kvt[:d], kvt[d:]
    kt_ref[0] = kt
    vt_ref[0] = vt
    key_block = ktb_ref.shape[3]
    for blk in range(ktb_ref.shape[1]):
        cols = slice(blk * key_block, (blk + 1) * key_block)
        ktb_ref[0, blk] = kt[:, cols].astype(BF16)
        vtb_ref[0, blk] = vt[:, cols].astype(BF16)
    qb_ref[...] = _dot((hn * gq_ref[...]).astype(BF16), wq_ref[...]).astype(BF16)


def _kvq(h, gain_kv, gain_q, w_kvt, w_q, *, batch, key_block):
    t_total, d = h.shape
    t = t_total // batch
    tm = _row_tile(t, 512)
    assert tm % key_block == 0, (tm, key_block)
    nt = t // tm
    n = w_q.shape[1]
    rows = lambda b, i: (b * nt + i, 0)
    full = lambda b, i: (0, 0)
    feat = pl.BlockSpec((1, n, tm), lambda b, i: (b, 0, i))
    blocks = pl.BlockSpec((1, tm // key_block, n, key_block), lambda b, i: (b, i, 0, 0))
    return pl.pallas_call(
        _kvq_kernel,
        grid=(batch, nt),
        in_specs=[pl.BlockSpec((tm, d), rows), pl.BlockSpec((1, d), full), pl.BlockSpec((1, d), full),
                  pl.BlockSpec((2 * n, d), full), pl.BlockSpec((d, n), full)],
        out_specs=[feat, feat, blocks, blocks, pl.BlockSpec((tm, n), rows)],
        out_shape=[jax.ShapeDtypeStruct((batch, n, t), F32), jax.ShapeDtypeStruct((batch, n, t), F32),
                   jax.ShapeDtypeStruct((batch, t // key_block, n, key_block), BF16),
                   jax.ShapeDtypeStruct((batch, t // key_block, n, key_block), BF16),
                   jax.ShapeDtypeStruct((t_total, n), BF16)],
        compiler_params=_compiler_params(("parallel", "parallel")),
        name="kvq_proj",
    )(h, gain_kv, gain_q, w_kvt, w_q)


def _suffix_matrix(tk, with_total):
    n = tk + (V7X_LANES if with_total else 0)
    j = lax.broadcasted_iota(jnp.int32, (tk, n), 0)
    s = lax.broadcasted_iota(jnp.int32, (tk, n), 1)
    return ((j >= s) | (s >= tk)).astype(BF16)


def _prompt_tile(pair, i, bias_ref, q_ref, kt_ref, vt_ref, suffix_ref, out_ref,
                 logw_scr, blocksum_scr, total_scr, acc_scr, *, tq, tk):
    n_rep = tk // V7X_LANES
    heads = range(HEADS_PER_LANE_TILE)
    lane = lax.broadcasted_iota(jnp.int32, (tq, V7X_LANES), 1)
    q_pair = q_ref[...].astype(F32)
    q_heads = [jnp.where(lane // SB_HEAD_DIM == e, q_pair, 0.0).astype(BF16) for e in heads]
    biases = [bias_ref[pair * HEADS_PER_LANE_TILE + e] for e in heads]
    suffix = suffix_ref[...]

    def score_rows(j, slot, r0, r1, diagonal):
        kt = kt_ref[0, j]
        if diagonal:
            row = lax.broadcasted_iota(jnp.int32, (r1 - r0, tk), 0)
            col = lax.broadcasted_iota(jnp.int32, (r1 - r0, tk), 1)
            visible = col < row
        for e in heads:
            z = _dot(q_heads[e][r0:r1], kt) + biases[e]
            sp = _softplus(z)
            if diagonal:
                sp = jnp.where(visible, sp, 0.0)
            log_w = z - _dot(sp.astype(BF16), suffix)
            if diagonal:
                log_w = jnp.where(visible, log_w, NEG_INF)
            logw_scr[slot, e, r0:r1, :] = log_w
            blocksum_scr[slot, e, r0:r1, :] = jnp.broadcast_to(jnp.sum(sp, axis=1, keepdims=True),
                                                               (r1 - r0, V7X_LANES))

    def weigh_rows(j, slot, r0):
        vt = vt_ref[0, j]
        for e in heads:
            total = total_scr[e, r0:, :]
            a = jnp.exp(logw_scr[slot, e, r0:, :] - jnp.concatenate([total] * n_rep, axis=1))
            acc_scr[e, r0:, :] += _dot_nt(a.astype(BF16), vt)
            total_scr[e, r0:, :] = total + blocksum_scr[slot, e, r0:, :]

    total_scr[...] = jnp.zeros_like(total_scr)
    acc_scr[...] = jnp.zeros_like(acc_scr)
    ratio = tq // tk
    base = i * ratio
    n_slots = logw_scr.shape[0]
    assert ratio % n_slots == 0

    def score_diagonal(d):
        score_rows(base + d, d % n_slots, d * tk, (d + 1) * tk, True)
        if (d + 1) * tk < tq:
            score_rows(base + d, d % n_slots, (d + 1) * tk, tq, False)

    score_diagonal(ratio - 1)
    for d in range(ratio - 1, 0, -1):
        weigh_rows(base + d, d % n_slots, d * tk)
        score_diagonal(d - 1)

    def body(it, carry):
        j = base - it * n_slots
        for u in range(n_slots):
            weigh_rows(j - u, (-u) % n_slots, 0)
            score_rows(j - u - 1, (-u - 1) % n_slots, 0, tq, False)
        return carry

    lax.fori_loop(0, base // n_slots, body, 0)
    weigh_rows(0, 0, 0)
    out = acc_scr[0]
    for e in heads[1:]:
        out = jnp.where(lane // SB_HEAD_DIM == e, acc_scr[e], out)
    out_ref[...] = out.astype(BF16)


def _paged_pages(first, last, q_ref, bias_ref, knew_ref, vnew_ref, k_refs, v_refs, out_ref,
                 qbd_scr, total_scr, acc_scr, *, n_q, page):
    n_rows = n_q * SB_HEADS
    d = q_ref.shape[2]
    n_rep = page // V7X_LANES
    row_d = lax.broadcasted_iota(jnp.int32, (SB_HEADS, d), 0)
    col_d = lax.broadcasted_iota(jnp.int32, (SB_HEADS, d), 1)
    head_cols = col_d // SB_HEAD_DIM == row_d

    j = lax.broadcasted_iota(jnp.int32, (page, page + V7X_LANES), 0)
    s = lax.broadcasted_iota(jnp.int32, (page, page + V7X_LANES), 1)
    suffix = ((j >= s) | (s >= page)).astype(BF16)

    def visit(pages, visible):
        logits = [_dot(qbd_scr[...], kt_page) + bias_ref[...] for kt_page, _ in pages]
        softplus = [_softplus(z) for z in logits]
        if visible is not None:
            softplus = [jnp.where(visible, sp, 0.0) for sp in softplus]
        sums = [_dot(sp.astype(BF16), suffix) for sp in softplus]
        log_ws = [z - s[:, :page] for z, s in zip(logits, sums)]
        if visible is not None:
            log_ws = [jnp.where(visible, lw, NEG_INF) for lw in log_ws]
        total = total_scr[...]
        weights = []
        for log_w, s in zip(log_ws, sums):
            weights.append(jnp.exp(log_w - jnp.concatenate([total] * n_rep, axis=1)).astype(BF16))
            total = total + s[:, page:]
        acc = acc_scr[...]
        for a, (_, vt_page) in zip(weights, pages):
            acc = acc + _dot_nt(a, vt_page)
        total_scr[...] = total
        acc_scr[...] = acc

    @pl.when(first)
    def _new_tokens():
        q = q_ref[0]
        blocks = [jnp.where(head_cols, jnp.broadcast_to(q[i:i + 1, :], (SB_HEADS, d)), 0.0)
                  for i in range(n_q)]
        qbd_scr[...] = jnp.concatenate(blocks, axis=0).astype(BF16)
        total_scr[...] = jnp.zeros_like(total_scr)
        acc_scr[...] = jnp.zeros_like(acc_scr)
        q_idx = lax.broadcasted_iota(jnp.int32, (n_rows, page), 0) // SB_HEADS
        key_idx = lax.broadcasted_iota(jnp.int32, (n_rows, page), 1)
        visit([(knew_ref[0], vnew_ref[0])], key_idx < q_idx)

    visit([(k_ref[0].astype(BF16), v_ref[0].astype(BF16)) for k_ref, v_ref in zip(k_refs, v_refs)], None)

    @pl.when(last)
    def _emit():
        acc = acc_scr[...]
        rows = []
        for i in range(n_q):
            blk = jnp.where(head_cols, acc[i * SB_HEADS:(i + 1) * SB_HEADS, :], 0.0)
            rows.append(jnp.sum(blk, axis=0, keepdims=True))
        out_ref[0] = jnp.concatenate(rows, axis=0).astype(BF16)


def _sb_attention_kernel(pt_ref, bias_ref, q_ref, kt_ref, vt_ref, suffix_ref,
                         qs_ref, bias_rows_ref, knew_ref, vnew_ref, *rest,
                         tq, tk, n_q, page, n_slots, steps_per_seq):
    del pt_ref
    k_refs, v_refs = rest[:n_slots], rest[n_slots:2 * n_slots]
    out_ref, outs_ref = rest[2 * n_slots:2 * n_slots + 2]
    logw_scr, blocksum_scr, total_scr, acc_scr, qbd_scr, totals_scr, accs_scr = rest[2 * n_slots + 2:]
    pair, i = pl.program_id(1), pl.program_id(2)
    step = (pl.program_id(0) * pl.num_programs(1) + pair) * pl.num_programs(2) + i
    group = lax.rem(step, steps_per_seq)
    _paged_pages(group == 0, group == steps_per_seq - 1, qs_ref, bias_rows_ref, knew_ref, vnew_ref,
                 k_refs, v_refs, outs_ref, qbd_scr, totals_scr, accs_scr, n_q=n_q, page=page)
    _prompt_tile(pair, i, bias_ref, q_ref, kt_ref, vt_ref, suffix_ref, out_ref,
                 logw_scr, blocksum_scr, total_scr, acc_scr, tq=tq, tk=tk)


def _sb_attention(q, kt_blocks, vt_blocks, bias, q_new, bias_rows, kt_new_pad, vt_new_pad,
                  cache_kt, cache_vt, page_table, *, batch):
    t_total, d = q.shape
    t = t_total // batch
    n_blocks, tk = kt_blocks.shape[1], kt_blocks.shape[3]
    tq = min(t, SB_QUERY_BLOCK)
    assert tq % tk == 0 and t % tq == 0, (t, tq, tk)
    nq = t // tq
    n_pairs = d // V7X_LANES
    n_steps = batch * n_pairs * nq

    seqs, n_q, _ = q_new.shape
    n_pages = page_table.shape[1]
    page = cache_kt.shape[2]
    n_rows = n_q * SB_HEADS
    assert (seqs * n_pages) % n_steps == 0, (seqs, n_pages, n_steps)
    n_slots = seqs * n_pages // n_steps
    assert n_pages % n_slots == 0, (n_pages, n_slots)
    steps_per_seq = n_pages // n_slots

    step_of = lambda b, p, i: (b * n_pairs + p) * nq + i
    seq_of = lambda b, p, i: step_of(b, p, i) // steps_per_seq
    per_seq = lambda b, p, i, pt: (seq_of(b, p, i), 0, 0)

    def page_spec(slot):
        def index(b, p, i, pt):
            logical = n_pages - 1 - (step_of(b, p, i) % steps_per_seq * n_slots + slot)
            return (pt[seq_of(b, p, i), logical], 0, 0)
        return pl.BlockSpec((1, d, page), index)

    kv_spec = pl.BlockSpec((1, n_blocks, V7X_LANES, tk), lambda b, p, i, pt: (b, 0, p, 0),
                           pipeline_mode=pl.Buffered(1))
    q_rows = pl.BlockSpec((tq, V7X_LANES), lambda b, p, i, pt: (b * nq + i, p))
    head_tiles = lambda lanes: pltpu.VMEM((HEADS_PER_LANE_TILE, tq, lanes), F32)
    loop_slots = math.gcd(tq // tk, SB_LOOP_UNROLL)
    slot_tiles = lambda lanes: pltpu.VMEM((loop_slots, HEADS_PER_LANE_TILE, tq, lanes), F32)
    grid_spec = pltpu.PrefetchScalarGridSpec(
        num_scalar_prefetch=1,
        grid=(batch, n_pairs, nq),
        in_specs=[
            pl.BlockSpec(memory_space=pltpu.SMEM),
            q_rows,
            kv_spec,
            kv_spec,
            pl.BlockSpec((tk, tk), lambda b, p, i, pt: (0, 0)),
            pl.BlockSpec((1, n_q, d), per_seq),
            pl.BlockSpec(bias_rows.shape, lambda b, p, i, pt: (0, 0)),
            pl.BlockSpec((1, d, page), per_seq),
            pl.BlockSpec((1, d, page), per_seq),
        ] + [page_spec(s) for s in range(n_slots)] * 2,
        out_specs=[q_rows, pl.BlockSpec((1, n_q, d), per_seq)],
        scratch_shapes=[slot_tiles(tk), slot_tiles(V7X_LANES), head_tiles(V7X_LANES), head_tiles(V7X_LANES),
                        pltpu.VMEM((n_rows, d), BF16), pltpu.VMEM((n_rows, V7X_LANES), F32),
                        pltpu.VMEM((n_rows, d), F32)],
    )
    return pl.pallas_call(
        functools.partial(_sb_attention_kernel, tq=tq, tk=tk, n_q=n_q, page=page, n_slots=n_slots,
                          steps_per_seq=steps_per_seq),
        grid_spec=grid_spec,
        out_shape=[jax.ShapeDtypeStruct((t_total, d), BF16), jax.ShapeDtypeStruct((seqs, n_q, d), BF16)],
        compiler_params=_compiler_params(("arbitrary", "arbitrary", "arbitrary")),
        name="sb_attention",
    )(page_table, bias, q, kt_blocks, vt_blocks, _suffix_matrix(tk, with_total=False),
      q_new, bias_rows, kt_new_pad, vt_new_pad, *([cache_kt] * n_slots), *([cache_vt] * n_slots))


def _prep_weights(norm_mix, norm_mlp, w_in_a, b_i, b_f, head_norm_a, w_out_a, norm_kv, w_kv, w_q_b,
                  w_out_b, sb_bias, w_up, w_down, norm_final):
    qk = MLSTM_HEADS * MLSTM_DK
    vd = MLSTM_HEADS * MLSTM_DV
    w_in = w_in_a[0]
    d = w_in.shape[0]
    gate_w = jnp.zeros((d, V7X_LANES), F32).at[:, :2 * MLSTM_HEADS].set(w_in[:, 2 * qk + 2 * vd:])
    gate_b = jnp.zeros((1, V7X_LANES), F32).at[0, :MLSTM_HEADS].set(b_i[0])
    gate_b = gate_b.at[0, MLSTM_HEADS:2 * MLSTM_HEADS].set(b_f[0])
    row = lambda g: g.reshape(1, -1).astype(F32)
    return dict(
        g_mix0=row(norm_mix[0]), g_mix1=row(norm_mix[1]), g_mlp0=row(norm_mlp[0]), g_mlp1=row(norm_mlp[1]),
        g_kv=row(norm_kv), g_final=row(norm_final), g_head=row(head_norm_a[0]),
        wq=w_in[:, :qk].astype(BF16),
        wkt=(w_in[:, qk:2 * qk] * (MLSTM_DK ** -0.5)).T.astype(BF16),
        wv=w_in[:, 2 * qk:2 * qk + vd].astype(BF16),
        wo=w_in[:, 2 * qk + vd:2 * qk + 2 * vd].astype(BF16),
        wg=gate_w.astype(BF16), gate_b=gate_b,
        w_out_a=w_out_a[0].astype(BF16),
        w_kvt=w_kv.T.astype(BF16),
        w_q=(w_q_b[0] * (SB_HEAD_DIM ** -0.5)).astype(BF16),
        w_out_b=w_out_b[0].astype(BF16),
        sb_bias=sb_bias[0].astype(F32),
        w_up0=w_up[0].astype(BF16), w_up1=w_up[1].astype(BF16),
        w_dn0=w_down[0].astype(BF16), w_dn1=w_down[1].astype(BF16),
    )


def _pack_state(c, n, m):
    n_rep = jnp.broadcast_to(n.astype(F32)[..., None], n.shape + (MLSTM_DV,))
    c_aug = jnp.concatenate([c.astype(F32), n_rep], axis=-1)
    return c_aug, jnp.broadcast_to(m.astype(F32)[..., None], m.shape + (V7X_LANES,))


def _unpack_state(c_aug, m_rep):
    return c_aug[..., :MLSTM_DV][None], c_aug[..., MLSTM_DV][None], m_rep[..., 0][None]


def _self_decoder(x2d, w, c_aug0, m0, *, batch, valid_len):
    q, kt, v, o, gates = _inproj(x2d, w["g_mix0"], w["wq"], w["wkt"], w["wv"], w["wo"], w["wg"])
    hg, c_aug, m_rep = _mlstm(q, kt, v, o, gates, w["gate_b"], w["g_head"], c_aug0, m0,
                              batch=batch, valid_len=valid_len)
    return hg, c_aug, m_rep


def kernel(x_prompt, x_sample, state_c, state_n, state_m, cache_k, cache_v, page_table, norm_mix, norm_mlp,
           w_in_a, b_i, b_f, head_norm_a, w_out_a, norm_kv, w_kv, w_q_b, w_out_b, sb_bias, w_up, w_down,
           norm_final):
    assert w_in_a.shape[0] == 1 and w_q_b.shape[0] == 1, "one self-decoder and one cross-decoder layer"
    w = _prep_weights(norm_mix, norm_mlp, w_in_a, b_i, b_f, head_norm_a, w_out_a, norm_kv, w_kv, w_q_b,
                      w_out_b, sb_bias, w_up, w_down, norm_final)
    bp, seq, d = x_prompt.shape
    db, dec_seq, _ = x_sample.shape
    n_pages, page = page_table.shape[1], cache_k.shape[1]
    kv_shape = (SB_HEADS, SB_HEAD_DIM)

    xp = x_prompt.reshape(bp * seq, d)
    zeros_c = jnp.zeros((bp, MLSTM_HEADS, MLSTM_DK, 2 * MLSTM_DV), F32)
    zeros_m = jnp.zeros((bp, MLSTM_HEADS, V7X_LANES), F32)
    hg, c_aug_p, m_rep_p = _self_decoder(xp, w, zeros_c, zeros_m, batch=bp, valid_len=MLSTM_CHUNK)
    h = _layer_tail(hg, w["w_out_a"], xp, w["g_mlp0"], w["w_up0"], w["w_dn0"], w["g_final"], final_norm=False)
    kt_p, vt_p, ktb, vtb, qb = _kvq(h, w["g_kv"], w["g_mix1"], w["w_kvt"], w["w_q"],
                                    batch=bp, key_block=min(seq, SB_KEY_BLOCK))

    n_new = db * dec_seq
    xs = x_sample.reshape(n_new, d)
    xs_pad = jnp.pad(x_sample, ((0, 0), (0, MLSTM_CHUNK - dec_seq), (0, 0))).reshape(db * MLSTM_CHUNK, d)
    c_aug0, m0 = _pack_state(state_c[0], state_n[0], state_m[0])
    hg_pad, c_aug_s, m_rep_s = _self_decoder(xs_pad, w, c_aug0, m0, batch=db, valid_len=dec_seq)
    hg_s = hg_pad.reshape(db, MLSTM_CHUNK, -1)[:, :dec_seq].reshape(n_new, -1)
    hs = _layer_tail(hg_s, w["w_out_a"], xs, w["g_mlp0"], w["w_up0"], w["w_dn0"], w["g_final"], final_norm=False)
    kt_s, vt_s, ktb_s, vtb_s, qb_s = _kvq(hs, w["g_kv"], w["g_mix1"], w["w_kvt"], w["w_q"],
                                          batch=1, key_block=n_new)

    new_page = lambda a: jnp.pad(a.reshape(d, db, dec_seq).transpose(1, 0, 2),
                                 ((0, 0), (0, 0), (0, page - dec_seq)))
    feature_major = lambda c: c.transpose(0, 2, 3, 1).reshape(c.shape[0], d, page)
    bias_rows = jnp.broadcast_to(jnp.tile(w["sb_bias"], dec_seq)[:, None], (dec_seq * SB_HEADS, page))
    attn, attn_s = _sb_attention(qb, ktb, vtb, w["sb_bias"],
                                 qb_s.astype(F32).reshape(db, dec_seq, d), bias_rows, new_page(ktb_s),
                                 new_page(vtb_s), feature_major(cache_k), feature_major(cache_v), page_table,
                                 batch=bp)

    y_p = _layer_tail(attn, w["w_out_b"], h, w["g_mlp1"], w["w_up1"], w["w_dn1"], w["g_final"], final_norm=True)
    y_s = _layer_tail(attn_s.reshape(n_new, d), w["w_out_b"], hs, w["g_mlp1"], w["w_up1"], w["w_dn1"],
                      w["g_final"], final_norm=True)
    c_p, n_p, m_p = _unpack_state(c_aug_p, m_rep_p)
    c_s, n_s, m_s = _unpack_state(c_aug_s, m_rep_s)
    token_major = lambda a: a.reshape(a.shape[0], *kv_shape, a.shape[2]).transpose(0, 3, 1, 2)
    new_rows = lambda a: a.reshape(*kv_shape, db, dec_seq).transpose(2, 3, 0, 1)

    return (y_p.reshape(bp, seq, d), y_s.reshape(db, dec_seq, d),
            c_p, n_p, m_p, token_major(kt_p), token_major(vt_p),
            c_s, n_s, m_s, new_rows(kt_s), new_rows(vt_s))
```

```python
import functools
import math

import jax
import jax.numpy as jnp
from jax import lax
from jax.experimental import pallas as pl
from jax.experimental.pallas import tpu as pltpu

F32 = jnp.float32
BF16 = jnp.bfloat16

EPS = 1e-6
LOG2E = 1.4426950408889634
MLSTM_HEADS = 8
MLSTM_DK = 64
MLSTM_DV = 128
MLSTM_CHUNK = 128
SB_HEADS = 16
SB_HEAD_DIM = 64

V7X_LANES = 128
V7X_VMEM_BYTES = 64 * 1024 * 1024
VMEM_LIMIT_BYTES = V7X_VMEM_BYTES - 8 * 1024 * 1024

HEADS_PER_LANE_TILE = V7X_LANES // SB_HEAD_DIM
SB_KEY_BLOCK = 256
SB_QUERY_BLOCK = 1024
MLSTM_SEQS_PER_STEP = 2
SB_LOOP_UNROLL = 4
NEG_INF = float("-inf")


def _compiler_params(semantics):
    return pltpu.CompilerParams(dimension_semantics=semantics, vmem_limit_bytes=VMEM_LIMIT_BYTES)


def _dot(a, b):
    return jnp.dot(a, b, preferred_element_type=F32)


def _dot_nt(a, b):
    return lax.dot_general(a, b, (((1,), (1,)), ((), ())), preferred_element_type=F32)


def _dot_exact(a, b):
    return jnp.dot(a, b, preferred_element_type=F32, precision=lax.Precision.HIGHEST)


def _rms_scale(x):
    return lax.rsqrt(jnp.mean(x * x, axis=-1, keepdims=True) + EPS)


def _softplus(z):
    return jnp.maximum(z, 0.0) + jnp.log(1.0 + jnp.exp2(jnp.abs(z) * (-LOG2E)))


def _row_tile(t, target):
    tile = min(t, target)
    assert t % tile == 0, (t, tile)
    return tile


def _inproj_kernel(x_ref, g_ref, wq_ref, wkt_ref, wv_ref, wo_ref, wg_ref,
                   q_ref, kt_ref, v_ref, o_ref, gate_ref):
    x = x_ref[...]
    xb = (x * _rms_scale(x) * g_ref[...]).astype(BF16)
    q_ref[...] = _dot(xb, wq_ref[...]).astype(BF16)
    kt_ref[...] = _dot_nt(wkt_ref[...], xb).astype(BF16)
    v_ref[...] = _dot(xb, wv_ref[...]).astype(BF16)
    o_ref[...] = _dot(xb, wo_ref[...])
    gate_ref[...] = _dot(xb, wg_ref[...])


def _inproj(x, gain, wq, wkt, wv, wo, wg):
    t, d = x.shape
    tm = _row_tile(t, 512)
    qk, vd, gw = wq.shape[1], wv.shape[1], wg.shape[1]
    full = lambda i: (0, 0)
    rows = lambda i: (i, 0)
    return pl.pallas_call(
        _inproj_kernel,
        grid=(t // tm,),
        in_specs=[
            pl.BlockSpec((tm, d), rows),
            pl.BlockSpec((1, d), full),
            pl.BlockSpec((d, qk), full),
            pl.BlockSpec((qk, d), full),
            pl.BlockSpec((d, vd), full),
            pl.BlockSpec((d, vd), full),
            pl.BlockSpec((d, gw), full),
        ],
        out_specs=[
            pl.BlockSpec((tm, qk), rows),
            pl.BlockSpec((qk, tm), lambda i: (0, i)),
            pl.BlockSpec((tm, vd), rows),
            pl.BlockSpec((tm, vd), rows),
            pl.BlockSpec((tm, gw), rows),
        ],
        out_shape=[
            jax.ShapeDtypeStruct((t, qk), BF16),
            jax.ShapeDtypeStruct((qk, t), BF16),
            jax.ShapeDtypeStruct((t, vd), BF16),
            jax.ShapeDtypeStruct((t, vd), F32),
            jax.ShapeDtypeStruct((t, gw), F32),
        ],
        compiler_params=_compiler_params(("parallel",)),
        name="mlstm_inproj",
    )(x, gain, wq, wkt, wv, wo, wg)


def _mlstm_kernel(q_ref, *rest, valid_len, n_seq):
    kt_refs = rest[:n_seq]
    v_ref, o_ref, g_ref, bias_ref, ghead_ref, c0_ref, m0_ref, hg_ref, c_ref, m_ref, rows_scr = rest[n_seq:]
    n_heads, dk, dv, chunk = MLSTM_HEADS, MLSTM_DK, MLSTM_DV, MLSTM_CHUNK
    assert chunk == V7X_LANES

    @pl.when(pl.program_id(1) == 0)
    def _load_state():
        c_ref[...] = c0_ref[...]
        m_ref[...] = m0_ref[...]

    lane = lax.broadcasted_iota(jnp.int32, (chunk, V7X_LANES), 1)
    sub = lax.broadcasted_iota(jnp.int32, (chunk, V7X_LANES), 0)
    is_f_lane = (lane >= n_heads) & (lane < 2 * n_heads)
    is_f_row = (sub >= n_heads) & (sub < 2 * n_heads)
    lower = (lane <= sub).astype(F32)
    upper = (sub <= lane).astype(F32)
    causal = lane <= sub
    ones_block = jnp.ones((chunk, dv), BF16)
    tile = lambda col: jnp.broadcast_to(col, (chunk, V7X_LANES))

    def gate_sums(s):
        gates = g_ref[s] + bias_ref[...]
        log_f = jnp.minimum(gates, 0.0) - jnp.log1p(jnp.exp(-jnp.abs(gates)))
        if valid_len < chunk:
            valid = sub < valid_len
            log_f = jnp.where(valid, log_f, 0.0)
            gates = jnp.where(valid, gates, NEG_INF)
        col_form = jnp.where(is_f_lane, log_f, jnp.where(lane < n_heads, gates, 0.0))
        row_form = col_form.T
        b_cols = _dot_exact(lower, jnp.where(is_f_lane, col_form, 0.0))
        b_rows = _dot_exact(jnp.where(is_f_row, row_form, 0.0), upper)
        rows_scr[s, 0:chunk, :] = row_form
        rows_scr[s, chunk:2 * chunk, :] = b_rows
        return b_cols

    def head_step(s, h, b_cols):
        pair, half = divmod(h, HEADS_PER_LANE_TILE)
        b_col = tile(jnp.sum(jnp.where(lane == n_heads + h, b_cols, 0.0), axis=1, keepdims=True))
        ig_row = rows_scr[s, h:h + 1, :]
        b_row = rows_scr[s, chunk + n_heads + h:chunk + n_heads + h + 1, :]
        m_prev = m_ref[s, h:h + 1, :]

        log_d = jnp.where(causal, b_col - b_row + ig_row, NEG_INF)
        m_inter = b_col + m_prev
        m_t = jnp.maximum(m_inter, tile(jnp.max(log_d, axis=1, keepdims=True)))
        w_inter = jnp.exp(m_inter - m_t)
        decay_mat = jnp.exp(log_d - m_t)

        q_pair = q_ref[s, :, pair * V7X_LANES:(pair + 1) * V7X_LANES].astype(F32)
        q_h = jnp.where(lane // dk == half, q_pair, 0.0).astype(BF16)
        kt_pair = kt_refs[s][pair * V7X_LANES:(pair + 1) * V7X_LANES, :]
        scores = _dot(q_h, kt_pair) * decay_mat

        v_aug = jnp.concatenate([v_ref[s, :, h * dv:(h + 1) * dv], ones_block], axis=1)
        c_h = c_ref[s, h]
        c_pair = jnp.concatenate([c_h, c_h], axis=0).astype(BF16)
        q_c = _dot(q_h, c_pair)
        s_v = _dot(scores.astype(BF16), v_aug)
        num = w_inter * q_c[:, :dv] + s_v[:, :dv]
        den = w_inter * q_c[:, dv:] + s_v[:, dv:]
        h_val = num / jnp.maximum(jnp.abs(den), jnp.exp(-m_t))

        mean_sq = tile(jnp.mean(h_val * h_val, axis=1, keepdims=True))
        y = h_val * lax.rsqrt(mean_sq + EPS) * ghead_ref[:, h * dv:(h + 1) * dv]
        gate_o = jax.nn.sigmoid(o_ref[s, :, h * dv:(h + 1) * dv])
        hg_ref[s, :, h * dv:(h + 1) * dv] = (gate_o * y).astype(BF16)

        m_new = m_t[chunk - 1:chunk, :]
        b_last = b_col[chunk - 1:chunk, :]
        w_k = jnp.exp(b_last - b_row + ig_row - m_new)
        decay = jnp.exp(b_last + m_prev - m_new)
        kt_h = kt_pair[half * dk:(half + 1) * dk, :].astype(F32)
        c_ref[s, h] = (jnp.concatenate([decay] * (2 * dv // V7X_LANES), axis=1) * c_h
                       + _dot((kt_h * w_k).astype(BF16), v_aug))
        m_ref[s, h:h + 1, :] = m_new

    b_cols = [gate_sums(s) for s in range(n_seq)]
    for h in range(n_heads):
        for s in range(n_seq):
            head_step(s, h, b_cols[s])


def _mlstm(q, kt, v, o, gates, gate_bias, ghead, c_aug0, m0, *, batch, valid_len):
    t_total = q.shape[0]
    chunk = MLSTM_CHUNK
    n_seq = MLSTM_SEQS_PER_STEP
    assert batch % n_seq == 0, (batch, n_seq)
    n_chunks = t_total // (batch * chunk)
    qk, vd, gw = q.shape[1], v.shape[1], gates.shape[1]
    per_seq = lambda a: a.reshape(batch, n_chunks * chunk, a.shape[1])
    rows = lambda b, c: (b, c, 0)
    full = lambda b, c: (0, 0)
    state4 = lambda b, c: (b, 0, 0, 0)
    state3 = lambda b, c: (b, 0, 0)
    kt_spec = lambda s: pl.BlockSpec((qk, chunk), lambda b, c: (0, (b * n_seq + s) * n_chunks + c))
    hg, c_aug, m_rep = pl.pallas_call(
        functools.partial(_mlstm_kernel, valid_len=valid_len, n_seq=n_seq),
        grid=(batch // n_seq, n_chunks),
        in_specs=[pl.BlockSpec((n_seq, chunk, qk), rows)] + [kt_spec(s) for s in range(n_seq)] + [
            pl.BlockSpec((n_seq, chunk, vd), rows),
            pl.BlockSpec((n_seq, chunk, vd), rows),
            pl.BlockSpec((n_seq, chunk, gw), rows),
            pl.BlockSpec((1, gw), full),
            pl.BlockSpec((1, vd), full),
            pl.BlockSpec((n_seq,) + c_aug0.shape[1:], state4),
            pl.BlockSpec((n_seq,) + m0.shape[1:], state3),
        ],
        out_specs=[
            pl.BlockSpec((n_seq, chunk, vd), rows),
            pl.BlockSpec((n_seq,) + c_aug0.shape[1:], state4),
            pl.BlockSpec((n_seq,) + m0.shape[1:], state3),
        ],
        out_shape=[
            jax.ShapeDtypeStruct((batch, n_chunks * chunk, vd), BF16),
            jax.ShapeDtypeStruct(c_aug0.shape, F32),
            jax.ShapeDtypeStruct(m0.shape, F32),
        ],
        scratch_shapes=[pltpu.VMEM((n_seq, 2 * chunk, V7X_LANES), F32)],
        compiler_params=_compiler_params(("parallel", "arbitrary")),
        name="mlstm_chunk",
    )(per_seq(q), *([kt] * n_seq), per_seq(v), per_seq(o), per_seq(gates), gate_bias, ghead, c_aug0, m0)
    return hg.reshape(t_total, vd), c_aug, m_rep


def _layer_tail_kernel(mix_ref, wout_ref, res_ref, g_ref, wup_ref, wdn_ref, gfin_ref, out_ref,
                       hn_scr, acc_scr, *, final_norm):
    j = pl.program_id(1)

    @pl.when(j == 0)
    def _start():
        h = res_ref[...] + _dot(mix_ref[...], wout_ref[...])
        hn_scr[...] = (h * _rms_scale(h) * g_ref[...]).astype(BF16)
        acc_scr[...] = h

    u = _dot(hn_scr[...], wup_ref[...])
    act = jnp.square(jnp.maximum(u, 0.0)).astype(BF16)
    acc_scr[...] += _dot(act, wdn_ref[...])

    @pl.when(j == pl.num_programs(1) - 1)
    def _finish():
        y = acc_scr[...]
        if final_norm:
            y = y * _rms_scale(y) * gfin_ref[...]
        out_ref[...] = y


def _layer_tail(mix, w_out, res, gain, w_up, w_down, gain_final, *, final_norm):
    t, d = res.shape
    k = mix.shape[1]
    ff = w_up.shape[1]
    tm = _row_tile(t, 1024)
    tf = _row_tile(ff, 1024)
    rows = lambda i, j: (i, 0)
    full = lambda i, j: (0, 0)
    return pl.pallas_call(
        functools.partial(_layer_tail_kernel, final_norm=final_norm),
        grid=(t // tm, ff // tf),
        in_specs=[
            pl.BlockSpec((tm, k), rows),
            pl.BlockSpec((k, d), full),
            pl.BlockSpec((tm, d), rows),
            pl.BlockSpec((1, d), full),
            pl.BlockSpec((d, tf), lambda i, j: (0, j)),
            pl.BlockSpec((tf, d), lambda i, j: (j, 0)),
            pl.BlockSpec((1, d), full),
        ],
        out_specs=pl.BlockSpec((tm, d), rows),
        out_shape=jax.ShapeDtypeStruct((t, d), F32),
        scratch_shapes=[pltpu.VMEM((tm, d), BF16), pltpu.VMEM((tm, d), F32)],
        compiler_params=_compiler_params(("parallel", "arbitrary")),
        name="layer_tail",
    )(mix, w_out, res, gain, w_up, w_down, gain_final)


def _kvq_kernel(h_ref, gkv_ref, gq_ref, wkvt_ref, wq_ref, kt_ref, vt_ref, ktb_ref, vtb_ref, qb_ref):
    h = h_ref[...]
    hn = h * _rms_scale(h)
    kvt = _dot_nt(wkvt_ref[...], (hn * gkv_ref[...]).astype(BF16))
    d = kt_ref.shape[1]
    kt, vt = kvt[:d], kvt[d:]
    kt_ref[0] = kt
    vt_ref[0] = vt
    key_block = ktb_ref.shape[3]
    for blk in range(ktb_ref.shape[1]):
        cols = slice(blk * key_block, (blk + 1) * key_block)
        ktb_ref[0, blk] = kt[:, cols].astype(BF16)
        vtb_ref[0, blk] = vt[:, cols].astype(BF16)
    qb_ref[...] = _dot((hn * gq_ref[...]).astype(BF16), wq_ref[...]).astype(BF16)


def _kvq(h, gain_kv, gain_q, w_kvt, w_q, *, batch, key_block):
    t_total, d = h.shape
    t = t_total // batch
    tm = _row_tile(t, 512)
    assert tm % key_block == 0, (tm, key_block)
    nt = t // tm
    n = w_q.shape[1]
    rows = lambda b, i: (b * nt + i, 0)
    full = lambda b, i: (0, 0)
    feat = pl.BlockSpec((1, n, tm), lambda b, i: (b, 0, i))
    blocks = pl.BlockSpec((1, tm // key_block, n, key_block), lambda b, i: (b, i, 0, 0))
    return pl.pallas_call(
        _kvq_kernel,
        grid=(batch, nt),
        in_specs=[pl.BlockSpec((tm, d), rows), pl.BlockSpec((1, d), full), pl.BlockSpec((1, d), full),
                  pl.BlockSpec((2 * n, d), full), pl.BlockSpec((d, n), full)],
        out_specs=[feat, feat, blocks, blocks, pl.BlockSpec((tm, n), rows)],
        out_shape=[jax.ShapeDtypeStruct((batch, n, t), F32), jax.ShapeDtypeStruct((batch, n, t), F32),
                   jax.ShapeDtypeStruct((batch, t // key_block, n, key_block), BF16),
                   jax.ShapeDtypeStruct((batch, t // key_block, n, key_block), BF16),
                   jax.ShapeDtypeStruct((t_total, n), BF16)],
        compiler_params=_compiler_params(("parallel", "parallel")),
        name="kvq_proj",
    )(h, gain_kv, gain_q, w_kvt, w_q)


def _suffix_matrix(tk, with_total):
    n = tk + (V7X_LANES if with_total else 0)
    j = lax.broadcasted_iota(jnp.int32, (tk, n), 0)
    s = lax.broadcasted_iota(jnp.int32, (tk, n), 1)
    return ((j >= s) | (s >= tk)).astype(BF16)


def _prompt_tile(pair, i, bias_ref, q_ref, kt_ref, vt_ref, suffix_ref, out_ref,
                 logw_scr, blocksum_scr, total_scr, acc_scr, *, tq, tk, between=()):
    pending = list(between)

    def run_next(keep=1):
        if len(pending) > keep:
            pending.pop(0)()
    n_rep = tk // V7X_LANES
    heads = range(HEADS_PER_LANE_TILE)
    lane = lax.broadcasted_iota(jnp.int32, (tq, V7X_LANES), 1)
    q_pair = q_ref[...].astype(F32)
    q_heads = [jnp.where(lane // SB_HEAD_DIM == e, q_pair, 0.0).astype(BF16) for e in heads]
    biases = [bias_ref[pair * HEADS_PER_LANE_TILE + e] for e in heads]
    suffix = suffix_ref[...]

    def score_rows(j, slot, r0, r1, diagonal):
        kt = kt_ref[0, j]
        if diagonal:
            row = lax.broadcasted_iota(jnp.int32, (r1 - r0, tk), 0)
            col = lax.broadcasted_iota(jnp.int32, (r1 - r0, tk), 1)
            visible = col < row
        for e in heads:
            z = _dot(q_heads[e][r0:r1], kt) + biases[e]
            sp = _softplus(z)
            if diagonal:
                sp = jnp.where(visible, sp, 0.0)
            suffix_sums = _dot(sp.astype(BF16), suffix)
            log_w = z - suffix_sums
            if diagonal:
                log_w = jnp.where(visible, log_w, NEG_INF)
            logw_scr[slot, e, r0:r1, :] = log_w
            blocksum_scr[slot, e, r0:r1, :] = jnp.broadcast_to(suffix_sums[:, 0:1], (r1 - r0, V7X_LANES))

    def weigh_rows(j, slot, r0):
        vt = vt_ref[0, j]
        for e in heads:
            total = total_scr[e, r0:, :]
            a = jnp.exp(logw_scr[slot, e, r0:, :] - jnp.concatenate([total] * n_rep, axis=1))
            acc_scr[e, r0:, :] += _dot_nt(a.astype(BF16), vt)
            total_scr[e, r0:, :] = total + blocksum_scr[slot, e, r0:, :]

    total_scr[...] = jnp.zeros_like(total_scr)
    acc_scr[...] = jnp.zeros_like(acc_scr)
    ratio = tq // tk
    base = i * ratio
    n_slots = logw_scr.shape[0]
    assert ratio % n_slots == 0

    def score_diagonal(d):
        score_rows(base + d, d % n_slots, d * tk, (d + 1) * tk, True)
        if (d + 1) * tk < tq:
            score_rows(base + d, d % n_slots, (d + 1) * tk, tq, False)

    run_next()
    score_diagonal(ratio - 1)
    run_next()
    for d in range(ratio - 1, 0, -1):
        weigh_rows(base + d, d % n_slots, d * tk)
        score_diagonal(d - 1)
        run_next()
    while len(pending) > 1:
        run_next()

    def body(it, carry):
        j = base - it * n_slots
        for u in range(n_slots):
            weigh_rows(j - u, (-u) % n_slots, 0)
            score_rows(j - u - 1, (-u - 1) % n_slots, 0, tq, False)
        return carry

    lax.fori_loop(0, base // n_slots, body, 0)
    weigh_rows(0, 0, 0)
    out = acc_scr[0]
    for e in heads[1:]:
        out = jnp.where(lane // SB_HEAD_DIM == e, acc_scr[e], out)
    out_ref[...] = out.astype(BF16)
    run_next(keep=0)


def _paged_pages(first, last, q_ref, bias_ref, knew_ref, vnew_ref, k_refs, v_refs, out_ref,
                 qbd_scr, total_scr, acc_scr, *, n_q, page):
    n_rows = n_q * SB_HEADS
    d = q_ref.shape[2]
    n_rep = page // V7X_LANES
    row_d = lax.broadcasted_iota(jnp.int32, (SB_HEADS, d), 0)
    col_d = lax.broadcasted_iota(jnp.int32, (SB_HEADS, d), 1)
    head_cols = col_d // SB_HEAD_DIM == row_d

    j = lax.broadcasted_iota(jnp.int32, (page, page + V7X_LANES), 0)
    s = lax.broadcasted_iota(jnp.int32, (page, page + V7X_LANES), 1)
    suffix = ((j >= s) | (s >= page)).astype(BF16)

    def visit_stages(pages, visible):
        state = {}

        def logits():
            state["z"] = [_dot(qbd_scr[...], kt_page) + bias_ref[...] for kt_page, _ in pages]

        def suffix_sums():
            softplus = [_softplus(z) for z in state["z"]]
            if visible is not None:
                softplus = [jnp.where(visible, sp, 0.0) for sp in softplus]
            state["sums"] = [_dot(sp.astype(BF16), suffix) for sp in softplus]

        def accumulate():
            log_ws = [z - s[:, :page] for z, s in zip(state["z"], state["sums"])]
            if visible is not None:
                log_ws = [jnp.where(visible, lw, NEG_INF) for lw in log_ws]
            total = total_scr[...]
            weights = []
            for log_w, s in zip(log_ws, state["sums"]):
                weights.append(jnp.exp(log_w - jnp.concatenate([total] * n_rep, axis=1)).astype(BF16))
                total = total + s[:, page:]
            acc = acc_scr[...]
            for a, (_, vt_page) in zip(weights, pages):
                acc = acc + _dot_nt(a, vt_page)
            total_scr[...] = total
            acc_scr[...] = acc

        return [logits, suffix_sums, accumulate]

    @pl.when(first)
    def _new_tokens():
        q = q_ref[0]
        blocks = [jnp.where(head_cols, jnp.broadcast_to(q[i:i + 1, :], (SB_HEADS, d)), 0.0)
                  for i in range(n_q)]
        qbd_scr[...] = jnp.concatenate(blocks, axis=0).astype(BF16)
        total_scr[...] = jnp.zeros_like(total_scr)
        acc_scr[...] = jnp.zeros_like(acc_scr)
        q_idx = lax.broadcasted_iota(jnp.int32, (n_rows, page), 0) // SB_HEADS
        key_idx = lax.broadcasted_iota(jnp.int32, (n_rows, page), 1)
        for stage in visit_stages([(knew_ref[0], vnew_ref[0])], key_idx < q_idx):
            stage()

    def emit():
        @pl.when(last)
        def _():
            acc = acc_scr[...]
            rows = []
            for i in range(n_q):
                blk = jnp.where(head_cols, acc[i * SB_HEADS:(i + 1) * SB_HEADS, :], 0.0)
                rows.append(jnp.sum(blk, axis=0, keepdims=True))
            out_ref[0] = jnp.concatenate(rows, axis=0).astype(BF16)

    cache_pages = [(k_ref[0].astype(BF16), v_ref[0].astype(BF16)) for k_ref, v_ref in zip(k_refs, v_refs)]
    return visit_stages(cache_pages, None) + [emit]


def _sb_attention_kernel(pt_ref, bias_ref, q_ref, kt_ref, vt_ref, suffix_ref,
                         qs_ref, bias_rows_ref, knew_ref, vnew_ref, *rest,
                         tq, tk, n_q, page, n_slots, steps_per_seq):
    del pt_ref
    k_refs, v_refs = rest[:n_slots], rest[n_slots:2 * n_slots]
    out_ref, outs_ref = rest[2 * n_slots:2 * n_slots + 2]
    logw_scr, blocksum_scr, total_scr, acc_scr, qbd_scr, totals_scr, accs_scr = rest[2 * n_slots + 2:]
    pair, i = pl.program_id(1), pl.program_id(2)
    step = (pl.program_id(0) * pl.num_programs(1) + pair) * pl.num_programs(2) + i
    group = lax.rem(step, steps_per_seq)
    sample_stages = _paged_pages(group == 0, group == steps_per_seq - 1, qs_ref, bias_rows_ref, knew_ref,
                                 vnew_ref, k_refs, v_refs, outs_ref, qbd_scr, totals_scr, accs_scr,
                                 n_q=n_q, page=page)
    _prompt_tile(pair, i, bias_ref, q_ref, kt_ref, vt_ref, suffix_ref, out_ref,
                 logw_scr, blocksum_scr, total_scr, acc_scr, tq=tq, tk=tk, between=sample_stages)


def _sb_attention(q, kt_blocks, vt_blocks, bias, q_new, bias_rows, kt_new_pad, vt_new_pad,
                  cache_kt, cache_vt, page_table, *, batch):
    t_total, d = q.shape
    t = t_total // batch
    n_blocks, tk = kt_blocks.shape[1], kt_blocks.shape[3]
    tq = min(t, SB_QUERY_BLOCK)
    assert tq % tk == 0 and t % tq == 0, (t, tq, tk)
    nq = t // tq
    n_pairs = d // V7X_LANES
    n_steps = batch * n_pairs * nq

    seqs, n_q, _ = q_new.shape
    n_pages = page_table.shape[1]
    page = cache_kt.shape[2]
    n_rows = n_q * SB_HEADS
    assert (seqs * n_pages) % n_steps == 0, (seqs, n_pages, n_steps)
    n_slots = seqs * n_pages // n_steps
    assert n_pages % n_slots == 0, (n_pages, n_slots)
    steps_per_seq = n_pages // n_slots

    step_of = lambda b, p, i: (b * n_pairs + p) * nq + i
    seq_of = lambda b, p, i: step_of(b, p, i) // steps_per_seq
    per_seq = lambda b, p, i, pt: (seq_of(b, p, i), 0, 0)

    def page_spec(slot):
        def index(b, p, i, pt):
            logical = n_pages - 1 - (step_of(b, p, i) % steps_per_seq * n_slots + slot)
            return (pt[seq_of(b, p, i), logical], 0, 0)
        return pl.BlockSpec((1, d, page), index)

    kv_spec = pl.BlockSpec((1, n_blocks, V7X_LANES, tk), lambda b, p, i, pt: (b, 0, p, 0),
                           pipeline_mode=pl.Buffered(1))
    q_rows = pl.BlockSpec((tq, V7X_LANES), lambda b, p, i, pt: (b * nq + i, p))
    head_tiles = lambda lanes: pltpu.VMEM((HEADS_PER_LANE_TILE, tq, lanes), F32)
    loop_slots = math.gcd(tq // tk, SB_LOOP_UNROLL)
    slot_tiles = lambda lanes: pltpu.VMEM((loop_slots, HEADS_PER_LANE_TILE, tq, lanes), F32)
    grid_spec = pltpu.PrefetchScalarGridSpec(
        num_scalar_prefetch=1,
        grid=(batch, n_pairs, nq),
        in_specs=[
            pl.BlockSpec(memory_space=pltpu.SMEM),
            q_rows,
            kv_spec,
            kv_spec,
            pl.BlockSpec((tk, tk), lambda b, p, i, pt: (0, 0)),
            pl.BlockSpec((1, n_q, d), per_seq),
            pl.BlockSpec(bias_rows.shape, lambda b, p, i, pt: (0, 0)),
            pl.BlockSpec((1, d, page), per_seq),
            pl.BlockSpec((1, d, page), per_seq),
        ] + [page_spec(s) for s in range(n_slots)] * 2,
        out_specs=[q_rows, pl.BlockSpec((1, n_q, d), per_seq)],
        scratch_shapes=[slot_tiles(tk), slot_tiles(V7X_LANES), head_tiles(V7X_LANES), head_tiles(V7X_LANES),
                        pltpu.VMEM((n_rows, d), BF16), pltpu.VMEM((n_rows, V7X_LANES), F32),
                        pltpu.VMEM((n_rows, d), F32)],
    )
    return pl.pallas_call(
        functools.partial(_sb_attention_kernel, tq=tq, tk=tk, n_q=n_q, page=page, n_slots=n_slots,
                          steps_per_seq=steps_per_seq),
        grid_spec=grid_spec,
        out_shape=[jax.ShapeDtypeStruct((t_total, d), BF16), jax.ShapeDtypeStruct((seqs, n_q, d), BF16)],
        compiler_params=_compiler_params(("arbitrary", "arbitrary", "arbitrary")),
        name="sb_attention",
    )(page_table, bias, q, kt_blocks, vt_blocks, _suffix_matrix(tk, with_total=False),
      q_new, bias_rows, kt_new_pad, vt_new_pad, *([cache_kt] * n_slots), *([cache_vt] * n_slots))


def _prep_weights(norm_mix, norm_mlp, w_in_a, b_i, b_f, head_norm_a, w_out_a, norm_kv, w_kv, w_q_b,
                  w_out_b, sb_bias, w_up, w_down, norm_final):
    qk = MLSTM_HEADS * MLSTM_DK
    vd = MLSTM_HEADS * MLSTM_DV
    w_in = w_in_a[0]
    d = w_in.shape[0]
    gate_w = jnp.zeros((d, V7X_LANES), F32).at[:, :2 * MLSTM_HEADS].set(w_in[:, 2 * qk + 2 * vd:])
    gate_b = jnp.zeros((1, V7X_LANES), F32).at[0, :MLSTM_HEADS].set(b_i[0])
    gate_b = gate_b.at[0, MLSTM_HEADS:2 * MLSTM_HEADS].set(b_f[0])
    row = lambda g: g.reshape(1, -1).astype(F32)
    return dict(
        g_mix0=row(norm_mix[0]), g_mix1=row(norm_mix[1]), g_mlp0=row(norm_mlp[0]), g_mlp1=row(norm_mlp[1]),
        g_kv=row(norm_kv), g_final=row(norm_final), g_head=row(head_norm_a[0]),
        wq=w_in[:, :qk].astype(BF16),
        wkt=(w_in[:, qk:2 * qk] * (MLSTM_DK ** -0.5)).T.astype(BF16),
        wv=w_in[:, 2 * qk:2 * qk + vd].astype(BF16),
        wo=w_in[:, 2 * qk + vd:2 * qk + 2 * vd].astype(BF16),
        wg=gate_w.astype(BF16), gate_b=gate_b,
        w_out_a=w_out_a[0].astype(BF16),
        w_kvt=w_kv.T.astype(BF16),
        w_q=(w_q_b[0] * (SB_HEAD_DIM ** -0.5)).astype(BF16),
        w_out_b=w_out_b[0].astype(BF16),
        sb_bias=sb_bias[0].astype(F32),
        w_up0=w_up[0].astype(BF16), w_up1=w_up[1].astype(BF16),
        w_dn0=w_down[0].astype(BF16), w_dn1=w_down[1].astype(BF16),
    )


def _pack_state(c, n, m):
    n_rep = jnp.broadcast_to(n.astype(F32)[..., None], n.shape + (MLSTM_DV,))
    c_aug = jnp.concatenate([c.astype(F32), n_rep], axis=-1)
    return c_aug, jnp.broadcast_to(m.astype(F32)[..., None], m.shape + (V7X_LANES,))


def _unpack_state(c_aug, m_rep):
    return c_aug[..., :MLSTM_DV][None], c_aug[..., MLSTM_DV][None], m_rep[..., 0][None]


def _self_decoder(x2d, w, c_aug0, m0, *, batch, valid_len):
    q, kt, v, o, gates = _inproj(x2d, w["g_mix0"], w["wq"], w["wkt"], w["wv"], w["wo"], w["wg"])
    hg, c_aug, m_rep = _mlstm(q, kt, v, o, gates, w["gate_b"], w["g_head"], c_aug0, m0,
                              batch=batch, valid_len=valid_len)
    return hg, c_aug, m_rep


def kernel(x_prompt, x_sample, state_c, state_n, state_m, cache_k, cache_v, page_table, norm_mix, norm_mlp,
           w_in_a, b_i, b_f, head_norm_a, w_out_a, norm_kv, w_kv, w_q_b, w_out_b, sb_bias, w_up, w_down,
           norm_final):
    assert w_in_a.shape[0] == 1 and w_q_b.shape[0] == 1, "one self-decoder and one cross-decoder layer"
    w = _prep_weights(norm_mix, norm_mlp, w_in_a, b_i, b_f, head_norm_a, w_out_a, norm_kv, w_kv, w_q_b,
                      w_out_b, sb_bias, w_up, w_down, norm_final)
    bp, seq, d = x_prompt.shape
    db, dec_seq, _ = x_sample.shape
    n_pages, page = page_table.shape[1], cache_k.shape[1]
    kv_shape = (SB_HEADS, SB_HEAD_DIM)

    xp = x_prompt.reshape(bp * seq, d)
    zeros_c = jnp.zeros((bp, MLSTM_HEADS, MLSTM_DK, 2 * MLSTM_DV), F32)
    zeros_m = jnp.zeros((bp, MLSTM_HEADS, V7X_LANES), F32)
    hg, c_aug_p, m_rep_p = _self_decoder(xp, w, zeros_c, zeros_m, batch=bp, valid_len=MLSTM_CHUNK)
    h = _layer_tail(hg, w["w_out_a"], xp, w["g_mlp0"], w["w_up0"], w["w_dn0"], w["g_final"], final_norm=False)
    kt_p, vt_p, ktb, vtb, qb = _kvq(h, w["g_kv"], w["g_mix1"], w["w_kvt"], w["w_q"],
                                    batch=bp, key_block=min(seq, SB_KEY_BLOCK))

    n_new = db * dec_seq
    xs = x_sample.reshape(n_new, d)
    xs_pad = jnp.pad(x_sample, ((0, 0), (0, MLSTM_CHUNK - dec_seq), (0, 0))).reshape(db * MLSTM_CHUNK, d)
    c_aug0, m0 = _pack_state(state_c[0], state_n[0], state_m[0])
    hg_pad, c_aug_s, m_rep_s = _self_decoder(xs_pad, w, c_aug0, m0, batch=db, valid_len=dec_seq)
    hg_s = hg_pad.reshape(db, MLSTM_CHUNK, -1)[:, :dec_seq].reshape(n_new, -1)
    hs = _layer_tail(hg_s, w["w_out_a"], xs, w["g_mlp0"], w["w_up0"], w["w_dn0"], w["g_final"], final_norm=False)
    kt_s, vt_s, ktb_s, vtb_s, qb_s = _kvq(hs, w["g_kv"], w["g_mix1"], w["w_kvt"], w["w_q"],
                                          batch=1, key_block=n_new)

    new_page = lambda a: jnp.pad(a.reshape(d, db, dec_seq).transpose(1, 0, 2),
                                 ((0, 0), (0, 0), (0, page - dec_seq)))
    feature_major = lambda c: c.transpose(0, 2, 3, 1).reshape(c.shape[0], d, page)
    bias_rows = jnp.broadcast_to(jnp.tile(w["sb_bias"], dec_seq)[:, None], (dec_seq * SB_HEADS, page))
    attn, attn_s = _sb_attention(qb, ktb, vtb, w["sb_bias"],
                                 qb_s.astype(F32).reshape(db, dec_seq, d), bias_rows, new_page(ktb_s),
                                 new_page(vtb_s), feature_major(cache_k), feature_major(cache_v), page_table,
                                 batch=bp)

    y_p = _layer_tail(attn, w["w_out_b"], h, w["g_mlp1"], w["w_up1"], w["w_dn1"], w["g_final"], final_norm=True)
    y_s = _layer_tail(attn_s.reshape(n_new, d), w["w_out_b"], hs, w["g_mlp1"], w["w_up1"], w["w_dn1"],
                      w["g_final"], final_norm=True)
    c_p, n_p, m_p = _unpack_state(c_aug_p, m_rep_p)
    c_s, n_s, m_s = _unpack_state(c_aug_s, m_rep_s)
    token_major = lambda a: a.reshape(a.shape[0], *kv_shape, a.shape[2]).transpose(0, 3, 1, 2)
    new_rows = lambda a: a.reshape(*kv_shape, db, dec_seq).transpose(2, 3, 0, 1)

    return (y_p.reshape(bp, seq, d), y_s.reshape(db, dec_seq, d),
            c_p, n_p, m_p, token_major(kt_p), token_major(vt_p),
            c_s, n_s, m_s, new_rows(kt_s), new_rows(vt_s))
```

```python
import functools
import math

import jax
import jax.numpy as jnp
from jax import lax
from jax.experimental import pallas as pl
from jax.experimental.pallas import tpu as pltpu

F32 = jnp.float32
BF16 = jnp.bfloat16

EPS = 1e-6
LOG2E = 1.4426950408889634
MLSTM_HEADS = 8
MLSTM_DK = 64
MLSTM_DV = 128
MLSTM_CHUNK = 128
SB_HEADS = 16
SB_HEAD_DIM = 64

V7X_LANES = 128
V7X_VMEM_BYTES = 64 * 1024 * 1024
VMEM_LIMIT_BYTES = V7X_VMEM_BYTES - 8 * 1024 * 1024

HEADS_PER_LANE_TILE = V7X_LANES // SB_HEAD_DIM
NEG_INF = float("-inf")

PROJ_ROW_TILE = 512
TAIL_ROW_TILE = 1024
TAIL_FF_TILE = 1024
SB_KEY_BLOCK = 256
SB_QUERY_BLOCK = 1024
SB_LOOP_UNROLL = 4
MLSTM_SEQS_PER_STEP = 2


def _compiler_params(semantics):
    return pltpu.CompilerParams(dimension_semantics=semantics, vmem_limit_bytes=VMEM_LIMIT_BYTES)


def _dot(a, b):
    return jnp.dot(a, b, preferred_element_type=F32)


def _dot_nt(a, b):
    return lax.dot_general(a, b, (((1,), (1,)), ((), ())), preferred_element_type=F32)


def _dot_exact(a, b):
    return jnp.dot(a, b, preferred_element_type=F32, precision=lax.Precision.HIGHEST)


def _rms_scale(x):
    return lax.rsqrt(jnp.mean(x * x, axis=-1, keepdims=True) + EPS)


def _softplus(z):
    return jnp.maximum(z, 0.0) + jnp.log(1.0 + jnp.exp2(jnp.abs(z) * (-LOG2E)))


def _row_tile(t, target):
    tile = min(t, target)
    assert t % tile == 0, (t, tile)
    return tile


def _inproj_kernel(x_ref, g_ref, wq_ref, wkt_ref, wv_ref, wo_ref, wg_ref,
                   q_ref, kt_ref, v_ref, o_ref, gate_ref):
    x = x_ref[...]
    xb = (x * _rms_scale(x) * g_ref[...]).astype(BF16)
    q_ref[...] = _dot(xb, wq_ref[...]).astype(BF16)
    kt_ref[...] = _dot_nt(wkt_ref[...], xb).astype(BF16)
    v_ref[...] = _dot(xb, wv_ref[...]).astype(BF16)
    o_ref[...] = _dot(xb, wo_ref[...])
    gate_ref[...] = _dot(xb, wg_ref[...])


def _inproj(x, gain, wq, wkt, wv, wo, wg):
    t, d = x.shape
    tm = _row_tile(t, PROJ_ROW_TILE)
    qk, vd, gw = wq.shape[1], wv.shape[1], wg.shape[1]
    full = lambda i: (0, 0)
    rows = lambda i: (i, 0)
    return pl.pallas_call(
        _inproj_kernel,
        grid=(t // tm,),
        in_specs=[
            pl.BlockSpec((tm, d), rows),
            pl.BlockSpec((1, d), full),
            pl.BlockSpec((d, qk), full),
            pl.BlockSpec((qk, d), full),
            pl.BlockSpec((d, vd), full),
            pl.BlockSpec((d, vd), full),
            pl.BlockSpec((d, gw), full),
        ],
        out_specs=[
            pl.BlockSpec((tm, qk), rows),
            pl.BlockSpec((qk, tm), lambda i: (0, i)),
            pl.BlockSpec((tm, vd), rows),
            pl.BlockSpec((tm, vd), rows),
            pl.BlockSpec((tm, gw), rows),
        ],
        out_shape=[
            jax.ShapeDtypeStruct((t, qk), BF16),
            jax.ShapeDtypeStruct((qk, t), BF16),
            jax.ShapeDtypeStruct((t, vd), BF16),
            jax.ShapeDtypeStruct((t, vd), F32),
            jax.ShapeDtypeStruct((t, gw), F32),
        ],
        compiler_params=_compiler_params(("parallel",)),
        name="mlstm_inproj",
    )(x, gain, wq, wkt, wv, wo, wg)


def _mlstm_kernel(q_ref, *rest, valid_len, n_seq):
    kt_refs = rest[:n_seq]
    v_ref, o_ref, g_ref, bias_ref, ghead_ref, c0_ref, m0_ref, hg_ref, c_ref, m_ref, rows_scr = rest[n_seq:]
    n_heads, dk, dv, chunk = MLSTM_HEADS, MLSTM_DK, MLSTM_DV, MLSTM_CHUNK
    assert chunk == V7X_LANES

    @pl.when(pl.program_id(1) == 0)
    def _load_state():
        c_ref[...] = c0_ref[...]
        m_ref[...] = m0_ref[...]

    lane = lax.broadcasted_iota(jnp.int32, (chunk, V7X_LANES), 1)
    sub = lax.broadcasted_iota(jnp.int32, (chunk, V7X_LANES), 0)
    is_f_lane = (lane >= n_heads) & (lane < 2 * n_heads)
    is_f_row = (sub >= n_heads) & (sub < 2 * n_heads)
    lower = (lane <= sub).astype(F32)
    upper = (sub <= lane).astype(F32)
    causal = lane <= sub
    ones_block = jnp.ones((chunk, dv), BF16)
    tile = lambda col: jnp.broadcast_to(col, (chunk, V7X_LANES))

    def gate_sums(s):
        gates = g_ref[s] + bias_ref[...]
        log_f = jnp.minimum(gates, 0.0) - jnp.log1p(jnp.exp(-jnp.abs(gates)))
        if valid_len < chunk:
            valid = sub < valid_len
            log_f = jnp.where(valid, log_f, 0.0)
            gates = jnp.where(valid, gates, NEG_INF)
        col_form = jnp.where(is_f_lane, log_f, jnp.where(lane < n_heads, gates, 0.0))
        row_form = col_form.T
        b_cols = _dot_exact(lower, jnp.where(is_f_lane, col_form, 0.0))
        b_rows = _dot_exact(jnp.where(is_f_row, row_form, 0.0), upper)
        rows_scr[s, 0:chunk, :] = row_form
        rows_scr[s, chunk:2 * chunk, :] = b_rows
        return b_cols

    def head_step(s, h, b_cols):
        pair, half = divmod(h, HEADS_PER_LANE_TILE)
        b_col = tile(jnp.sum(jnp.where(lane == n_heads + h, b_cols, 0.0), axis=1, keepdims=True))
        ig_row = rows_scr[s, h:h + 1, :]
        b_row = rows_scr[s, chunk + n_heads + h:chunk + n_heads + h + 1, :]
        m_prev = m_ref[s, h:h + 1, :]

        log_d = jnp.where(causal, b_col - b_row + ig_row, NEG_INF)
        m_inter = b_col + m_prev
        m_t = jnp.maximum(m_inter, tile(jnp.max(log_d, axis=1, keepdims=True)))
        w_inter = jnp.exp(m_inter - m_t)
        decay_mat = jnp.exp(log_d - m_t)

        q_pair = q_ref[s, :, pair * V7X_LANES:(pair + 1) * V7X_LANES].astype(F32)
        q_h = jnp.where(lane // dk == half, q_pair, 0.0).astype(BF16)
        kt_pair = kt_refs[s][pair * V7X_LANES:(pair + 1) * V7X_LANES, :]
        scores = _dot(q_h, kt_pair) * decay_mat

        v_aug = jnp.concatenate([v_ref[s, :, h * dv:(h + 1) * dv], ones_block], axis=1)
        c_h = c_ref[s, h]
        c_pair = jnp.concatenate([c_h, c_h], axis=0).astype(BF16)
        q_c = _dot(q_h, c_pair)
        s_v = _dot(scores.astype(BF16), v_aug)
        num = w_inter * q_c[:, :dv] + s_v[:, :dv]
        den = w_inter * q_c[:, dv:] + s_v[:, dv:]
        h_val = num / jnp.maximum(jnp.abs(den), jnp.exp(-m_t))

        mean_sq = tile(jnp.mean(h_val * h_val, axis=1, keepdims=True))
        y = h_val * lax.rsqrt(mean_sq + EPS) * ghead_ref[:, h * dv:(h + 1) * dv]
        gate_o = jax.nn.sigmoid(o_ref[s, :, h * dv:(h + 1) * dv])
        hg_ref[s, :, h * dv:(h + 1) * dv] = (gate_o * y).astype(BF16)

        m_new = m_t[chunk - 1:chunk, :]
        b_last = b_col[chunk - 1:chunk, :]
        w_k = jnp.exp(b_last - b_row + ig_row - m_new)
        decay = jnp.exp(b_last + m_prev - m_new)
        kt_h = kt_pair[half * dk:(half + 1) * dk, :].astype(F32)
        c_ref[s, h] = (jnp.concatenate([decay] * (2 * dv // V7X_LANES), axis=1) * c_h
                       + _dot((kt_h * w_k).astype(BF16), v_aug))
        m_ref[s, h:h + 1, :] = m_new

    b_cols = [gate_sums(s) for s in range(n_seq)]
    for h in range(n_heads):
        for s in range(n_seq):
            head_step(s, h, b_cols[s])


def _mlstm(q, kt, v, o, gates, gate_bias, ghead, c_aug0, m0, *, batch, valid_len):
    t_total = q.shape[0]
    chunk = MLSTM_CHUNK
    n_seq = MLSTM_SEQS_PER_STEP
    assert batch % n_seq == 0, (batch, n_seq)
    n_chunks = t_total // (batch * chunk)
    qk, vd, gw = q.shape[1], v.shape[1], gates.shape[1]
    per_seq = lambda a: a.reshape(batch, n_chunks * chunk, a.shape[1])
    rows = lambda b, c: (b, c, 0)
    full = lambda b, c: (0, 0)
    state4 = lambda b, c: (b, 0, 0, 0)
    state3 = lambda b, c: (b, 0, 0)
    kt_spec = lambda s: pl.BlockSpec((qk, chunk), lambda b, c: (0, (b * n_seq + s) * n_chunks + c))
    hg, c_aug, m_rep = pl.pallas_call(
        functools.partial(_mlstm_kernel, valid_len=valid_len, n_seq=n_seq),
        grid=(batch // n_seq, n_chunks),
        in_specs=[pl.BlockSpec((n_seq, chunk, qk), rows)] + [kt_spec(s) for s in range(n_seq)] + [
            pl.BlockSpec((n_seq, chunk, vd), rows),
            pl.BlockSpec((n_seq, chunk, vd), rows),
            pl.BlockSpec((n_seq, chunk, gw), rows),
            pl.BlockSpec((1, gw), full),
            pl.BlockSpec((1, vd), full),
            pl.BlockSpec((n_seq,) + c_aug0.shape[1:], state4),
            pl.BlockSpec((n_seq,) + m0.shape[1:], state3),
        ],
        out_specs=[
            pl.BlockSpec((n_seq, chunk, vd), rows),
            pl.BlockSpec((n_seq,) + c_aug0.shape[1:], state4),
            pl.BlockSpec((n_seq,) + m0.shape[1:], state3),
        ],
        out_shape=[
            jax.ShapeDtypeStruct((batch, n_chunks * chunk, vd), BF16),
            jax.ShapeDtypeStruct(c_aug0.shape, F32),
            jax.ShapeDtypeStruct(m0.shape, F32),
        ],
        scratch_shapes=[pltpu.VMEM((n_seq, 2 * chunk, V7X_LANES), F32)],
        compiler_params=_compiler_params(("parallel", "arbitrary")),
        name="mlstm_chunk",
    )(per_seq(q), *([kt] * n_seq), per_seq(v), per_seq(o), per_seq(gates), gate_bias, ghead, c_aug0, m0)
    return hg.reshape(t_total, vd), c_aug, m_rep


def _layer_tail_kernel(mix_ref, wout_ref, res_ref, g_ref, wup_ref, wdn_ref, gfin_ref, out_ref,
                       hn_scr, acc_scr, *, final_norm):
    j = pl.program_id(1)

    @pl.when(j == 0)
    def _start():
        h = res_ref[...] + _dot(mix_ref[...], wout_ref[...])
        hn_scr[...] = (h * _rms_scale(h) * g_ref[...]).astype(BF16)
        acc_scr[...] = h

    u = _dot(hn_scr[...], wup_ref[...])
    act = jnp.square(jnp.maximum(u, 0.0)).astype(BF16)
    acc_scr[...] += _dot(act, wdn_ref[...])

    @pl.when(j == pl.num_programs(1) - 1)
    def _finish():
        y = acc_scr[...]
        if final_norm:
            y = y * _rms_scale(y) * gfin_ref[...]
        out_ref[...] = y


def _layer_tail(mix, w_out, res, gain, w_up, w_down, gain_final, *, final_norm):
    t, d = res.shape
    k = mix.shape[1]
    ff = w_up.shape[1]
    tm = _row_tile(t, TAIL_ROW_TILE)
    tf = _row_tile(ff, TAIL_FF_TILE)
    rows = lambda i, j: (i, 0)
    full = lambda i, j: (0, 0)
    return pl.pallas_call(
        functools.partial(_layer_tail_kernel, final_norm=final_norm),
        grid=(t // tm, ff // tf),
        in_specs=[
            pl.BlockSpec((tm, k), rows),
            pl.BlockSpec((k, d), full),
            pl.BlockSpec((tm, d), rows),
            pl.BlockSpec((1, d), full),
            pl.BlockSpec((d, tf), lambda i, j: (0, j)),
            pl.BlockSpec((tf, d), lambda i, j: (j, 0)),
            pl.BlockSpec((1, d), full),
        ],
        out_specs=pl.BlockSpec((tm, d), rows),
        out_shape=jax.ShapeDtypeStruct((t, d), F32),
        scratch_shapes=[pltpu.VMEM((tm, d), BF16), pltpu.VMEM((tm, d), F32)],
        compiler_params=_compiler_params(("parallel", "arbitrary")),
        name="layer_tail",
    )(mix, w_out, res, gain, w_up, w_down, gain_final)


def _kvq_kernel(h_ref, gkv_ref, gq_ref, wkvt_ref, wq_ref, kt_ref, vt_ref, ktb_ref, vtb_ref, qb_ref):
    h = h_ref[...]
    hn = h * _rms_scale(h)
    kvt = _dot_nt(wkvt_ref[...], (hn * gkv_ref[...]).astype(BF16))
    d = kt_ref.shape[1]
    kt, vt = kvt[:d], kvt[d:]
    kt_ref[0] = kt
    vt_ref[0] = vt
    key_block = ktb_ref.shape[3]
    for blk in range(ktb_ref.shape[1]):
        cols = slice(blk * key_block, (blk + 1) * key_block)
        ktb_ref[0, blk] = kt[:, cols].astype(BF16)
        vtb_ref[0, blk] = vt[:, cols].astype(BF16)
    qb_ref[...] = _dot((hn * gq_ref[...]).astype(BF16), wq_ref[...]).astype(BF16)


def _kvq(h, gain_kv, gain_q, w_kvt, w_q, *, batch, key_block):
    t_total, d = h.shape
    t = t_total // batch
    tm = _row_tile(t, PROJ_ROW_TILE)
    assert tm % key_block == 0, (tm, key_block)
    nt = t // tm
    n = w_q.shape[1]
    rows = lambda b, i: (b * nt + i, 0)
    full = lambda b, i: (0, 0)
    feat = pl.BlockSpec((1, n, tm), lambda b, i: (b, 0, i))
    blocks = pl.BlockSpec((1, tm // key_block, n, key_block), lambda b, i: (b, i, 0, 0))
    return pl.pallas_call(
        _kvq_kernel,
        grid=(batch, nt),
        in_specs=[pl.BlockSpec((tm, d), rows), pl.BlockSpec((1, d), full), pl.BlockSpec((1, d), full),
                  pl.BlockSpec((2 * n, d), full), pl.BlockSpec((d, n), full)],
        out_specs=[feat, feat, blocks, blocks, pl.BlockSpec((tm, n), rows)],
        out_shape=[jax.ShapeDtypeStruct((batch, n, t), F32), jax.ShapeDtypeStruct((batch, n, t), F32),
                   jax.ShapeDtypeStruct((batch, t // key_block, n, key_block), BF16),
                   jax.ShapeDtypeStruct((batch, t // key_block, n, key_block), BF16),
                   jax.ShapeDtypeStruct((t_total, n), BF16)],
        compiler_params=_compiler_params(("parallel", "parallel")),
        name="kvq_proj",
    )(h, gain_kv, gain_q, w_kvt, w_q)


def _suffix_matrix(tk):
    j = lax.broadcasted_iota(jnp.int32, (tk, tk), 0)
    s = lax.broadcasted_iota(jnp.int32, (tk, tk), 1)
    return (j >= s).astype(BF16)


def _prompt_tile(pair, i, bias_ref, q_ref, kt_ref, vt_ref, suffix_ref, out_ref,
                 logw_scr, blocksum_scr, total_scr, acc_scr, *, tq, tk, between=()):
    pending = list(between)

    def run_next(keep=1):
        if len(pending) > keep:
            pending.pop(0)()
    n_rep = tk // V7X_LANES
    heads = range(HEADS_PER_LANE_TILE)
    lane = lax.broadcasted_iota(jnp.int32, (tq, V7X_LANES), 1)
    q_pair = q_ref[...].astype(F32)
    q_heads = [jnp.where(lane // SB_HEAD_DIM == e, q_pair, 0.0).astype(BF16) for e in heads]
    biases = [bias_ref[pair * HEADS_PER_LANE_TILE + e] for e in heads]
    suffix = suffix_ref[...]

    def score_rows(j, slot, r0, r1, diagonal):
        kt = kt_ref[0, j]
        if diagonal:
            row = lax.broadcasted_iota(jnp.int32, (r1 - r0, tk), 0)
            col = lax.broadcasted_iota(jnp.int32, (r1 - r0, tk), 1)
            visible = col < row
        for e in heads:
            z = _dot(q_heads[e][r0:r1], kt) + biases[e]
            sp = _softplus(z)
            if diagonal:
                sp = jnp.where(visible, sp, 0.0)
            suffix_sums = _dot(sp.astype(BF16), suffix)
            log_w = z - suffix_sums
            if diagonal:
                log_w = jnp.where(visible, log_w, NEG_INF)
            logw_scr[slot, e, r0:r1, :] = log_w
            blocksum_scr[slot, e, r0:r1, :] = jnp.broadcast_to(suffix_sums[:, 0:1], (r1 - r0, V7X_LANES))

    def weigh_rows(j, slot, r0):
        vt = vt_ref[0, j]
        for e in heads:
            total = total_scr[e, r0:, :]
            a = jnp.exp(logw_scr[slot, e, r0:, :] - jnp.concatenate([total] * n_rep, axis=1))
            acc_scr[e, r0:, :] += _dot_nt(a.astype(BF16), vt)
            total_scr[e, r0:, :] = total + blocksum_scr[slot, e, r0:, :]

    total_scr[...] = jnp.zeros_like(total_scr)
    acc_scr[...] = jnp.zeros_like(acc_scr)
    ratio = tq // tk
    base = i * ratio
    n_slots = logw_scr.shape[0]
    assert ratio % n_slots == 0

    def score_diagonal(d):
        score_rows(base + d, d % n_slots, d * tk, (d + 1) * tk, True)
        if (d + 1) * tk < tq:
            score_rows(base + d, d % n_slots, (d + 1) * tk, tq, False)

    run_next()
    score_diagonal(ratio - 1)
    run_next()
    for d in range(ratio - 1, 0, -1):
        weigh_rows(base + d, d % n_slots, d * tk)
        score_diagonal(d - 1)
        run_next()
    while len(pending) > 1:
        run_next()

    def body(it, carry):
        j = base - it * n_slots
        for u in range(n_slots):
            weigh_rows(j - u, (-u) % n_slots, 0)
            score_rows(j - u - 1, (-u - 1) % n_slots, 0, tq, False)
        return carry

    lax.fori_loop(0, base // n_slots, body, 0)
    weigh_rows(0, 0, 0)
    out = acc_scr[0]
    for e in heads[1:]:
        out = jnp.where(lane // SB_HEAD_DIM == e, acc_scr[e], out)
    out_ref[...] = out.astype(BF16)
    run_next(keep=0)


def _paged_pages(first, last, q_ref, bias_ref, knew_ref, vnew_ref, k_refs, v_refs, out_ref,
                 qbd_scr, total_scr, acc_scr, *, n_q, page):
    n_rows = n_q * SB_HEADS
    d = q_ref.shape[2]
    n_rep = page // V7X_LANES
    row_d = lax.broadcasted_iota(jnp.int32, (SB_HEADS, d), 0)
    col_d = lax.broadcasted_iota(jnp.int32, (SB_HEADS, d), 1)
    head_cols = col_d // SB_HEAD_DIM == row_d

    j = lax.broadcasted_iota(jnp.int32, (page, page + V7X_LANES), 0)
    s = lax.broadcasted_iota(jnp.int32, (page, page + V7X_LANES), 1)
    suffix = ((j >= s) | (s >= page)).astype(BF16)

    def visit_stages(pages, visible):
        state = {}

        def logits():
            state["z"] = [_dot(qbd_scr[...], kt_page) + bias_ref[...] for kt_page, _ in pages]

        def suffix_sums():
            softplus = [_softplus(z) for z in state["z"]]
            if visible is not None:
                softplus = [jnp.where(visible, sp, 0.0) for sp in softplus]
            state["sums"] = [_dot(sp.astype(BF16), suffix) for sp in softplus]

        def accumulate():
            log_ws = [z - s[:, :page] for z, s in zip(state["z"], state["sums"])]
            if visible is not None:
                log_ws = [jnp.where(visible, lw, NEG_INF) for lw in log_ws]
            total = total_scr[...]
            weights = []
            for log_w, s in zip(log_ws, state["sums"]):
                weights.append(jnp.exp(log_w - jnp.concatenate([total] * n_rep, axis=1)).astype(BF16))
                total = total + s[:, page:]
            acc = acc_scr[...]
            for a, (_, vt_page) in zip(weights, pages):
                acc = acc + _dot_nt(a, vt_page)
            total_scr[...] = total
            acc_scr[...] = acc

        return [logits, suffix_sums, accumulate]

    @pl.when(first)
    def _new_tokens():
        q = q_ref[0]
        blocks = [jnp.where(head_cols, jnp.broadcast_to(q[i:i + 1, :], (SB_HEADS, d)), 0.0)
                  for i in range(n_q)]
        qbd_scr[...] = jnp.concatenate(blocks, axis=0).astype(BF16)
        total_scr[...] = jnp.zeros_like(total_scr)
        acc_scr[...] = jnp.zeros_like(acc_scr)
        q_idx = lax.broadcasted_iota(jnp.int32, (n_rows, page), 0) // SB_HEADS
        key_idx = lax.broadcasted_iota(jnp.int32, (n_rows, page), 1)
        for stage in visit_stages([(knew_ref[0], vnew_ref[0])], key_idx < q_idx):
            stage()

    def emit():
        @pl.when(last)
        def _():
            acc = acc_scr[...]
            rows = []
            for i in range(n_q):
                blk = jnp.where(head_cols, acc[i * SB_HEADS:(i + 1) * SB_HEADS, :], 0.0)
                rows.append(jnp.sum(blk, axis=0, keepdims=True))
            out_ref[0] = jnp.concatenate(rows, axis=0).astype(BF16)

    cache_pages = [(k_ref[0].astype(BF16), v_ref[0].astype(BF16)) for k_ref, v_ref in zip(k_refs, v_refs)]
    return visit_stages(cache_pages, None) + [emit]


def _sb_attention_kernel(pt_ref, bias_ref, q_ref, kt_ref, vt_ref, suffix_ref,
                         qs_ref, bias_rows_ref, knew_ref, vnew_ref, *rest,
                         tq, tk, n_q, page, n_slots, steps_per_seq):
    del pt_ref
    k_refs, v_refs = rest[:n_slots], rest[n_slots:2 * n_slots]
    out_ref, outs_ref = rest[2 * n_slots:2 * n_slots + 2]
    logw_scr, blocksum_scr, total_scr, acc_scr, qbd_scr, totals_scr, accs_scr = rest[2 * n_slots + 2:]
    pair, i = pl.program_id(1), pl.program_id(2)
    step = (pl.program_id(0) * pl.num_programs(1) + pair) * pl.num_programs(2) + i
    group = lax.rem(step, steps_per_seq)
    sample_stages = _paged_pages(group == 0, group == steps_per_seq - 1, qs_ref, bias_rows_ref, knew_ref,
                                 vnew_ref, k_refs, v_refs, outs_ref, qbd_scr, totals_scr, accs_scr,
                                 n_q=n_q, page=page)
    _prompt_tile(pair, i, bias_ref, q_ref, kt_ref, vt_ref, suffix_ref, out_ref,
                 logw_scr, blocksum_scr, total_scr, acc_scr, tq=tq, tk=tk, between=sample_stages)


def _sb_attention(q, kt_blocks, vt_blocks, bias, q_new, bias_rows, kt_new_pad, vt_new_pad,
                  cache_kt, cache_vt, page_table, *, batch):
    t_total, d = q.shape
    t = t_total // batch
    n_blocks, tk = kt_blocks.shape[1], kt_blocks.shape[3]
    tq = min(t, SB_QUERY_BLOCK)
    assert tq % tk == 0 and t % tq == 0, (t, tq, tk)
    nq = t // tq
    n_pairs = d // V7X_LANES
    n_steps = batch * n_pairs * nq

    seqs, n_q, _ = q_new.shape
    n_pages = page_table.shape[1]
    page = cache_kt.shape[2]
    n_rows = n_q * SB_HEADS
    assert (seqs * n_pages) % n_steps == 0, (seqs, n_pages, n_steps)
    n_slots = seqs * n_pages // n_steps
    assert n_pages % n_slots == 0, (n_pages, n_slots)
    steps_per_seq = n_pages // n_slots

    step_of = lambda b, p, i: (b * n_pairs + p) * nq + i
    seq_of = lambda b, p, i: lax.div(step_of(b, p, i), steps_per_seq)
    per_seq = lambda b, p, i, pt: (seq_of(b, p, i), 0, 0)

    def page_spec(slot):
        def index(b, p, i, pt):
            logical = n_pages - 1 - (lax.rem(step_of(b, p, i), steps_per_seq) * n_slots + slot)
            return (pt[seq_of(b, p, i), logical], 0, 0)
        return pl.BlockSpec((1, d, page), index)

    kv_spec = pl.BlockSpec((1, n_blocks, V7X_LANES, tk), lambda b, p, i, pt: (b, 0, p, 0),
                           pipeline_mode=pl.Buffered(1))
    q_rows = pl.BlockSpec((tq, V7X_LANES), lambda b, p, i, pt: (b * nq + i, p))
    head_tiles = lambda lanes: pltpu.VMEM((HEADS_PER_LANE_TILE, tq, lanes), F32)
    loop_slots = math.gcd(tq // tk, SB_LOOP_UNROLL)
    slot_tiles = lambda lanes: pltpu.VMEM((loop_slots, HEADS_PER_LANE_TILE, tq, lanes), F32)
    grid_spec = pltpu.PrefetchScalarGridSpec(
        num_scalar_prefetch=1,
        grid=(batch, n_pairs, nq),
        in_specs=[
            pl.BlockSpec(memory_space=pltpu.SMEM),
            q_rows,
            kv_spec,
            kv_spec,
            pl.BlockSpec((tk, tk), lambda b, p, i, pt: (0, 0)),
            pl.BlockSpec((1, n_q, d), per_seq),
            pl.BlockSpec(bias_rows.shape, lambda b, p, i, pt: (0, 0)),
            pl.BlockSpec((1, d, page), per_seq),
            pl.BlockSpec((1, d, page), per_seq),
        ] + [page_spec(s) for s in range(n_slots)] * 2,
        out_specs=[q_rows, pl.BlockSpec((1, n_q, d), per_seq)],
        scratch_shapes=[slot_tiles(tk), slot_tiles(V7X_LANES), head_tiles(V7X_LANES), head_tiles(V7X_LANES),
                        pltpu.VMEM((n_rows, d), BF16), pltpu.VMEM((n_rows, V7X_LANES), F32),
                        pltpu.VMEM((n_rows, d), F32)],
    )
    return pl.pallas_call(
        functools.partial(_sb_attention_kernel, tq=tq, tk=tk, n_q=n_q, page=page, n_slots=n_slots,
                          steps_per_seq=steps_per_seq),
        grid_spec=grid_spec,
        out_shape=[jax.ShapeDtypeStruct((t_total, d), BF16), jax.ShapeDtypeStruct((seqs, n_q, d), BF16)],
        compiler_params=_compiler_params(("arbitrary", "arbitrary", "arbitrary")),
        name="sb_attention",
    )(page_table, bias, q, kt_blocks, vt_blocks, _suffix_matrix(tk),
      q_new, bias_rows, kt_new_pad, vt_new_pad, *([cache_kt] * n_slots), *([cache_vt] * n_slots))


def _prep_weights(norm_mix, norm_mlp, w_in_a, b_i, b_f, head_norm_a, w_out_a, norm_kv, w_kv, w_q_b,
                  w_out_b, sb_bias, w_up, w_down, norm_final):
    qk = MLSTM_HEADS * MLSTM_DK
    vd = MLSTM_HEADS * MLSTM_DV
    w_in = w_in_a[0]
    d = w_in.shape[0]
    gate_w = jnp.zeros((d, V7X_LANES), F32).at[:, :2 * MLSTM_HEADS].set(w_in[:, 2 * qk + 2 * vd:])
    gate_b = jnp.zeros((1, V7X_LANES), F32).at[0, :MLSTM_HEADS].set(b_i[0])
    gate_b = gate_b.at[0, MLSTM_HEADS:2 * MLSTM_HEADS].set(b_f[0])
    row = lambda g: g.reshape(1, -1).astype(F32)
    return dict(
        g_mix0=row(norm_mix[0]), g_mix1=row(norm_mix[1]), g_mlp0=row(norm_mlp[0]), g_mlp1=row(norm_mlp[1]),
        g_kv=row(norm_kv), g_final=row(norm_final), g_head=row(head_norm_a[0]),
        wq=w_in[:, :qk].astype(BF16),
        wkt=(w_in[:, qk:2 * qk] * (MLSTM_DK ** -0.5)).T.astype(BF16),
        wv=w_in[:, 2 * qk:2 * qk + vd].astype(BF16),
        wo=w_in[:, 2 * qk + vd:2 * qk + 2 * vd].astype(BF16),
        wg=gate_w.astype(BF16), gate_b=gate_b,
        w_out_a=w_out_a[0].astype(BF16),
        w_kvt=w_kv.T.astype(BF16),
        w_q=(w_q_b[0] * (SB_HEAD_DIM ** -0.5)).astype(BF16),
        w_out_b=w_out_b[0].astype(BF16),
        sb_bias=sb_bias[0].astype(F32),
        w_up0=w_up[0].astype(BF16), w_up1=w_up[1].astype(BF16),
        w_dn0=w_down[0].astype(BF16), w_dn1=w_down[1].astype(BF16),
    )


def _pack_state(c, n, m):
    n_rep = jnp.broadcast_to(n.astype(F32)[..., None], n.shape + (MLSTM_DV,))
    c_aug = jnp.concatenate([c.astype(F32), n_rep], axis=-1)
    return c_aug, jnp.broadcast_to(m.astype(F32)[..., None], m.shape + (V7X_LANES,))


def _unpack_state(c_aug, m_rep):
    return c_aug[..., :MLSTM_DV][None], c_aug[..., MLSTM_DV][None], m_rep[..., 0][None]


def _self_decoder(x2d, w, c_aug0, m0, *, batch, valid_len):
    q, kt, v, o, gates = _inproj(x2d, w["g_mix0"], w["wq"], w["wkt"], w["wv"], w["wo"], w["wg"])
    hg, c_aug, m_rep = _mlstm(q, kt, v, o, gates, w["gate_b"], w["g_head"], c_aug0, m0,
                              batch=batch, valid_len=valid_len)
    return hg, c_aug, m_rep


def kernel(x_prompt, x_sample, state_c, state_n, state_m, cache_k, cache_v, page_table, norm_mix, norm_mlp,
           w_in_a, b_i, b_f, head_norm_a, w_out_a, norm_kv, w_kv, w_q_b, w_out_b, sb_bias, w_up, w_down,
           norm_final):
    assert w_in_a.shape[0] == 1 and w_q_b.shape[0] == 1, "one self-decoder and one cross-decoder layer"
    w = _prep_weights(norm_mix, norm_mlp, w_in_a, b_i, b_f, head_norm_a, w_out_a, norm_kv, w_kv, w_q_b,
                      w_out_b, sb_bias, w_up, w_down, norm_final)
    bp, seq, d = x_prompt.shape
    db, dec_seq, _ = x_sample.shape
    n_pages, page = page_table.shape[1], cache_k.shape[1]
    kv_shape = (SB_HEADS, SB_HEAD_DIM)

    xp = x_prompt.reshape(bp * seq, d)
    zeros_c = jnp.zeros((bp, MLSTM_HEADS, MLSTM_DK, 2 * MLSTM_DV), F32)
    zeros_m = jnp.zeros((bp, MLSTM_HEADS, V7X_LANES), F32)
    hg, c_aug_p, m_rep_p = _self_decoder(xp, w, zeros_c, zeros_m, batch=bp, valid_len=MLSTM_CHUNK)
    h = _layer_tail(hg, w["w_out_a"], xp, w["g_mlp0"], w["w_up0"], w["w_dn0"], w["g_final"], final_norm=False)
    kt_p, vt_p, ktb, vtb, qb = _kvq(h, w["g_kv"], w["g_mix1"], w["w_kvt"], w["w_q"],
                                    batch=bp, key_block=min(seq, SB_KEY_BLOCK))

    n_new = db * dec_seq
    xs = x_sample.reshape(n_new, d)
    xs_pad = jnp.pad(x_sample, ((0, 0), (0, MLSTM_CHUNK - dec_seq), (0, 0))).reshape(db * MLSTM_CHUNK, d)
    c_aug0, m0 = _pack_state(state_c[0], state_n[0], state_m[0])
    hg_pad, c_aug_s, m_rep_s = _self_decoder(xs_pad, w, c_aug0, m0, batch=db, valid_len=dec_seq)
    hg_s = hg_pad.reshape(db, MLSTM_CHUNK, -1)[:, :dec_seq].reshape(n_new, -1)
    hs = _layer_tail(hg_s, w["w_out_a"], xs, w["g_mlp0"], w["w_up0"], w["w_dn0"], w["g_final"], final_norm=False)
    kt_s, vt_s, ktb_s, vtb_s, qb_s = _kvq(hs, w["g_kv"], w["g_mix1"], w["w_kvt"], w["w_q"],
                                          batch=1, key_block=n_new)

    new_page = lambda a: jnp.pad(a.reshape(d, db, dec_seq).transpose(1, 0, 2),
                                 ((0, 0), (0, 0), (0, page - dec_seq)))
    feature_major = lambda c: c.transpose(0, 2, 3, 1).reshape(c.shape[0], d, page)
    bias_rows = jnp.broadcast_to(jnp.tile(w["sb_bias"], dec_seq)[:, None], (dec_seq * SB_HEADS, page))
    attn, attn_s = _sb_attention(qb, ktb, vtb, w["sb_bias"],
                                 qb_s.astype(F32).reshape(db, dec_seq, d), bias_rows, new_page(ktb_s),
                                 new_page(vtb_s), feature_major(cache_k), feature_major(cache_v), page_table,
                                 batch=bp)

    y_p = _layer_tail(attn, w["w_out_b"], h, w["g_mlp1"], w["w_up1"], w["w_dn1"], w["g_final"], final_norm=True)
    y_s = _layer_tail(attn_s.reshape(n_new, d), w["w_out_b"], hs, w["g_mlp1"], w["w_up1"], w["w_dn1"],
                      w["g_final"], final_norm=True)
    c_p, n_p, m_p = _unpack_state(c_aug_p, m_rep_p)
    c_s, n_s, m_s = _unpack_state(c_aug_s, m_rep_s)
    token_major = lambda a: a.reshape(a.shape[0], *kv_shape, a.shape[2]).transpose(0, 3, 1, 2)
    new_rows = lambda a: a.reshape(*kv_shape, db, dec_seq).transpose(2, 3, 0, 1)

    return (y_p.reshape(bp, seq, d), y_s.reshape(db, dec_seq, d),
            c_p, n_p, m_p, token_major(kt_p), token_major(vt_p),
            c_s, n_s, m_s, new_rows(kt_s), new_rows(vt_s))
```

```python
import functools
import math

import jax
import jax.numpy as jnp
from jax import lax
from jax.experimental import pallas as pl
from jax.experimental.pallas import tpu as pltpu

F32 = jnp.float32
BF16 = jnp.bfloat16

EPS = 1e-6
LOG2E = 1.4426950408889634
MLSTM_HEADS = 8
MLSTM_DK = 64
MLSTM_DV = 128
MLSTM_CHUNK = 128
SB_HEADS = 16
SB_HEAD_DIM = 64

V7X_LANES = 128
V7X_VMEM_BYTES = 64 * 1024 * 1024
VMEM_LIMIT_BYTES = V7X_VMEM_BYTES - 8 * 1024 * 1024

HEADS_PER_LANE_TILE = V7X_LANES // SB_HEAD_DIM
NEG_INF = float("-inf")

PROJ_ROW_TILE = 512
TAIL_ROW_TILE = 1024
TAIL_FF_TILE = 1024
SB_KEY_BLOCK = 256
SB_QUERY_BLOCK = 1024
SB_LOOP_UNROLL = 4
MLSTM_SEQS_PER_STEP = 2
PAGE_GROUPS_PER_STEP = 4


def _compiler_params(semantics):
    return pltpu.CompilerParams(dimension_semantics=semantics, vmem_limit_bytes=VMEM_LIMIT_BYTES)


def _dot(a, b):
    return jnp.dot(a, b, preferred_element_type=F32)


def _dot_nt(a, b):
    return lax.dot_general(a, b, (((1,), (1,)), ((), ())), preferred_element_type=F32)


def _dot_exact(a, b):
    return jnp.dot(a, b, preferred_element_type=F32, precision=lax.Precision.HIGHEST)


def _rms_scale(x):
    return lax.rsqrt(jnp.mean(x * x, axis=-1, keepdims=True) + EPS)


def _softplus(z):
    return jnp.maximum(z, 0.0) + jnp.log(1.0 + jnp.exp2(jnp.abs(z) * (-LOG2E)))


def _row_tile(t, target):
    tile = min(t, target)
    assert t % tile == 0, (t, tile)
    return tile


def _inproj_kernel(x_ref, g_ref, wq_ref, wkt_ref, wv_ref, wo_ref, wg_ref,
                   q_ref, kt_ref, v_ref, o_ref, gate_ref):
    x = x_ref[...]
    xb = (x * _rms_scale(x) * g_ref[...]).astype(BF16)
    q_ref[...] = _dot(xb, wq_ref[...]).astype(BF16)
    kt_ref[...] = _dot_nt(wkt_ref[...], xb).astype(BF16)
    v_ref[...] = _dot(xb, wv_ref[...]).astype(BF16)
    o_ref[...] = _dot(xb, wo_ref[...])
    gate_ref[...] = _dot(xb, wg_ref[...])


def _inproj(x, gain, wq, wkt, wv, wo, wg):
    t, d = x.shape
    tm = _row_tile(t, PROJ_ROW_TILE)
    qk, vd, gw = wq.shape[1], wv.shape[1], wg.shape[1]
    full = lambda i: (0, 0)
    rows = lambda i: (i, 0)
    return pl.pallas_call(
        _inproj_kernel,
        grid=(t // tm,),
        in_specs=[
            pl.BlockSpec((tm, d), rows),
            pl.BlockSpec((1, d), full),
            pl.BlockSpec((d, qk), full),
            pl.BlockSpec((qk, d), full),
            pl.BlockSpec((d, vd), full),
            pl.BlockSpec((d, vd), full),
            pl.BlockSpec((d, gw), full),
        ],
        out_specs=[
            pl.BlockSpec((tm, qk), rows),
            pl.BlockSpec((qk, tm), lambda i: (0, i)),
            pl.BlockSpec((tm, vd), rows),
            pl.BlockSpec((tm, vd), rows),
            pl.BlockSpec((tm, gw), rows),
        ],
        out_shape=[
            jax.ShapeDtypeStruct((t, qk), BF16),
            jax.ShapeDtypeStruct((qk, t), BF16),
            jax.ShapeDtypeStruct((t, vd), BF16),
            jax.ShapeDtypeStruct((t, vd), F32),
            jax.ShapeDtypeStruct((t, gw), F32),
        ],
        compiler_params=_compiler_params(("parallel",)),
        name="mlstm_inproj",
    )(x, gain, wq, wkt, wv, wo, wg)


def _mlstm_kernel(q_ref, *rest, valid_len, n_seq):
    kt_refs = rest[:n_seq]
    v_ref, o_ref, g_ref, bias_ref, ghead_ref, c0_ref, m0_ref, hg_ref, c_ref, m_ref, rows_scr = rest[n_seq:]
    n_heads, dk, dv, chunk = MLSTM_HEADS, MLSTM_DK, MLSTM_DV, MLSTM_CHUNK
    assert chunk == V7X_LANES

    @pl.when(pl.program_id(1) == 0)
    def _load_state():
        c_ref[...] = c0_ref[...]
        m_ref[...] = m0_ref[...]

    lane = lax.broadcasted_iota(jnp.int32, (chunk, V7X_LANES), 1)
    sub = lax.broadcasted_iota(jnp.int32, (chunk, V7X_LANES), 0)
    is_f_lane = (lane >= n_heads) & (lane < 2 * n_heads)
    is_f_row = (sub >= n_heads) & (sub < 2 * n_heads)
    lower = (lane <= sub).astype(F32)
    upper = (sub <= lane).astype(F32)
    causal = lane <= sub
    ones_block = jnp.ones((chunk, dv), BF16)
    tile = lambda col: jnp.broadcast_to(col, (chunk, V7X_LANES))

    def gate_sums(s):
        gates = g_ref[s] + bias_ref[...]
        log_f = jnp.minimum(gates, 0.0) - jnp.log1p(jnp.exp(-jnp.abs(gates)))
        if valid_len < chunk:
            valid = sub < valid_len
            log_f = jnp.where(valid, log_f, 0.0)
            gates = jnp.where(valid, gates, NEG_INF)
        col_form = jnp.where(is_f_lane, log_f, jnp.where(lane < n_heads, gates, 0.0))
        row_form = col_form.T
        b_cols = _dot_exact(lower, jnp.where(is_f_lane, col_form, 0.0))
        b_rows = _dot_exact(jnp.where(is_f_row, row_form, 0.0), upper)
        rows_scr[s, 0:chunk, :] = row_form
        rows_scr[s, chunk:2 * chunk, :] = b_rows
        return b_cols

    def head_step(s, h, b_cols):
        pair, half = divmod(h, HEADS_PER_LANE_TILE)
        b_col = tile(jnp.sum(jnp.where(lane == n_heads + h, b_cols, 0.0), axis=1, keepdims=True))
        ig_row = rows_scr[s, h:h + 1, :]
        b_row = rows_scr[s, chunk + n_heads + h:chunk + n_heads + h + 1, :]
        m_prev = m_ref[s, h:h + 1, :]

        log_d = jnp.where(causal, b_col - b_row + ig_row, NEG_INF)
        m_inter = b_col + m_prev
        m_t = jnp.maximum(m_inter, tile(jnp.max(log_d, axis=1, keepdims=True)))
        w_inter = jnp.exp(m_inter - m_t)
        decay_mat = jnp.exp(log_d - m_t)

        q_pair = q_ref[s, :, pair * V7X_LANES:(pair + 1) * V7X_LANES].astype(F32)
        q_h = jnp.where(lane // dk == half, q_pair, 0.0).astype(BF16)
        kt_pair = kt_refs[s][pair * V7X_LANES:(pair + 1) * V7X_LANES, :]
        scores = _dot(q_h, kt_pair) * decay_mat

        v_aug = jnp.concatenate([v_ref[s, :, h * dv:(h + 1) * dv], ones_block], axis=1)
        c_h = c_ref[s, h]
        c_pair = jnp.concatenate([c_h, c_h], axis=0).astype(BF16)
        q_c = _dot(q_h, c_pair)
        s_v = _dot(scores.astype(BF16), v_aug)
        num = w_inter * q_c[:, :dv] + s_v[:, :dv]
        den = w_inter * q_c[:, dv:] + s_v[:, dv:]
        h_val = num / jnp.maximum(jnp.abs(den), jnp.exp(-m_t))

        mean_sq = tile(jnp.mean(h_val * h_val, axis=1, keepdims=True))
        y = h_val * lax.rsqrt(mean_sq + EPS) * ghead_ref[:, h * dv:(h + 1) * dv]
        gate_o = jax.nn.sigmoid(o_ref[s, :, h * dv:(h + 1) * dv])
        hg_ref[s, :, h * dv:(h + 1) * dv] = (gate_o * y).astype(BF16)

        m_new = m_t[chunk - 1:chunk, :]
        b_last = b_col[chunk - 1:chunk, :]
        w_k = jnp.exp(b_last - b_row + ig_row - m_new)
        decay = jnp.exp(b_last + m_prev - m_new)
        kt_h = kt_pair[half * dk:(half + 1) * dk, :].astype(F32)
        c_ref[s, h] = (jnp.concatenate([decay] * (2 * dv // V7X_LANES), axis=1) * c_h
                       + _dot((kt_h * w_k).astype(BF16), v_aug))
        m_ref[s, h:h + 1, :] = m_new

    b_cols = [gate_sums(s) for s in range(n_seq)]
    for h in range(n_heads):
        for s in range(n_seq):
            head_step(s, h, b_cols[s])


def _mlstm(q, kt, v, o, gates, gate_bias, ghead, c_aug0, m0, *, batch, valid_len):
    t_total = q.shape[0]
    chunk = MLSTM_CHUNK
    n_seq = MLSTM_SEQS_PER_STEP
    assert batch % n_seq == 0, (batch, n_seq)
    n_chunks = t_total // (batch * chunk)
    qk, vd, gw = q.shape[1], v.shape[1], gates.shape[1]
    per_seq = lambda a: a.reshape(batch, n_chunks * chunk, a.shape[1])
    rows = lambda b, c: (b, c, 0)
    full = lambda b, c: (0, 0)
    state4 = lambda b, c: (b, 0, 0, 0)
    state3 = lambda b, c: (b, 0, 0)
    kt_spec = lambda s: pl.BlockSpec((qk, chunk), lambda b, c: (0, (b * n_seq + s) * n_chunks + c))
    hg, c_aug, m_rep = pl.pallas_call(
        functools.partial(_mlstm_kernel, valid_len=valid_len, n_seq=n_seq),
        grid=(batch // n_seq, n_chunks),
        in_specs=[pl.BlockSpec((n_seq, chunk, qk), rows)] + [kt_spec(s) for s in range(n_seq)] + [
            pl.BlockSpec((n_seq, chunk, vd), rows),
            pl.BlockSpec((n_seq, chunk, vd), rows),
            pl.BlockSpec((n_seq, chunk, gw), rows),
            pl.BlockSpec((1, gw), full),
            pl.BlockSpec((1, vd), full),
            pl.BlockSpec((n_seq,) + c_aug0.shape[1:], state4),
            pl.BlockSpec((n_seq,) + m0.shape[1:], state3),
        ],
        out_specs=[
            pl.BlockSpec((n_seq, chunk, vd), rows),
            pl.BlockSpec((n_seq,) + c_aug0.shape[1:], state4),
            pl.BlockSpec((n_seq,) + m0.shape[1:], state3),
        ],
        out_shape=[
            jax.ShapeDtypeStruct((batch, n_chunks * chunk, vd), BF16),
            jax.ShapeDtypeStruct(c_aug0.shape, F32),
            jax.ShapeDtypeStruct(m0.shape, F32),
        ],
        scratch_shapes=[pltpu.VMEM((n_seq, 2 * chunk, V7X_LANES), F32)],
        compiler_params=_compiler_params(("parallel", "arbitrary")),
        name="mlstm_chunk",
    )(per_seq(q), *([kt] * n_seq), per_seq(v), per_seq(o), per_seq(gates), gate_bias, ghead, c_aug0, m0)
    return hg.reshape(t_total, vd), c_aug, m_rep


def _layer_tail_kernel(mix_ref, wout_ref, res_ref, g_ref, wup_ref, wdn_ref, gfin_ref, out_ref,
                       hn_scr, acc_scr, *, final_norm):
    j = pl.program_id(1)

    @pl.when(j == 0)
    def _start():
        h = res_ref[...] + _dot(mix_ref[...], wout_ref[...])
        hn_scr[...] = (h * _rms_scale(h) * g_ref[...]).astype(BF16)
        acc_scr[...] = h

    u = _dot(hn_scr[...], wup_ref[...])
    act = jnp.square(jnp.maximum(u, 0.0)).astype(BF16)
    acc_scr[...] += _dot(act, wdn_ref[...])

    @pl.when(j == pl.num_programs(1) - 1)
    def _finish():
        y = acc_scr[...]
        if final_norm:
            y = y * _rms_scale(y) * gfin_ref[...]
        out_ref[...] = y


def _layer_tail(mix, w_out, res, gain, w_up, w_down, gain_final, *, final_norm):
    t, d = res.shape
    k = mix.shape[1]
    ff = w_up.shape[1]
    tm = _row_tile(t, TAIL_ROW_TILE)
    tf = _row_tile(ff, TAIL_FF_TILE)
    rows = lambda i, j: (i, 0)
    full = lambda i, j: (0, 0)
    return pl.pallas_call(
        functools.partial(_layer_tail_kernel, final_norm=final_norm),
        grid=(t // tm, ff // tf),
        in_specs=[
            pl.BlockSpec((tm, k), rows),
            pl.BlockSpec((k, d), full),
            pl.BlockSpec((tm, d), rows),
            pl.BlockSpec((1, d), full),
            pl.BlockSpec((d, tf), lambda i, j: (0, j)),
            pl.BlockSpec((tf, d), lambda i, j: (j, 0)),
            pl.BlockSpec((1, d), full),
        ],
        out_specs=pl.BlockSpec((tm, d), rows),
        out_shape=jax.ShapeDtypeStruct((t, d), F32),
        scratch_shapes=[pltpu.VMEM((tm, d), BF16), pltpu.VMEM((tm, d), F32)],
        compiler_params=_compiler_params(("parallel", "arbitrary")),
        name="layer_tail",
    )(mix, w_out, res, gain, w_up, w_down, gain_final)


def _kvq_kernel(h_ref, gkv_ref, gq_ref, wkvt_ref, wq_ref, kt_ref, vt_ref, ktb_ref, vtb_ref, qb_ref):
    h = h_ref[...]
    hn = h * _rms_scale(h)
    kvt = _dot_nt(wkvt_ref[...], (hn * gkv_ref[...]).astype(BF16))
    d = kt_ref.shape[1]
    kt, vt = kvt[:d], kvt[d:]
    kt_ref[0] = kt
    vt_ref[0] = vt
    key_block = ktb_ref.shape[3]
    for blk in range(ktb_ref.shape[1]):
        cols = slice(blk * key_block, (blk + 1) * key_block)
        ktb_ref[0, blk] = kt[:, cols].astype(BF16)
        vtb_ref[0, blk] = vt[:, cols].astype(BF16)
    qb_ref[...] = _dot((hn * gq_ref[...]).astype(BF16), wq_ref[...]).astype(BF16)


def _kvq(h, gain_kv, gain_q, w_kvt, w_q, *, batch, key_block):
    t_total, d = h.shape
    t = t_total // batch
    tm = _row_tile(t, PROJ_ROW_TILE)
    assert tm % key_block == 0, (tm, key_block)
    nt = t // tm
    n = w_q.shape[1]
    rows = lambda b, i: (b * nt + i, 0)
    full = lambda b, i: (0, 0)
    feat = pl.BlockSpec((1, n, tm), lambda b, i: (b, 0, i))
    blocks = pl.BlockSpec((1, tm // key_block, n, key_block), lambda b, i: (b, i, 0, 0))
    return pl.pallas_call(
        _kvq_kernel,
        grid=(batch, nt),
        in_specs=[pl.BlockSpec((tm, d), rows), pl.BlockSpec((1, d), full), pl.BlockSpec((1, d), full),
                  pl.BlockSpec((2 * n, d), full), pl.BlockSpec((d, n), full)],
        out_specs=[feat, feat, blocks, blocks, pl.BlockSpec((tm, n), rows)],
        out_shape=[jax.ShapeDtypeStruct((batch, n, t), F32), jax.ShapeDtypeStruct((batch, n, t), F32),
                   jax.ShapeDtypeStruct((batch, t // key_block, n, key_block), BF16),
                   jax.ShapeDtypeStruct((batch, t // key_block, n, key_block), BF16),
                   jax.ShapeDtypeStruct((t_total, n), BF16)],
        compiler_params=_compiler_params(("parallel", "parallel")),
        name="kvq_proj",
    )(h, gain_kv, gain_q, w_kvt, w_q)


def _suffix_matrix(tk):
    j = lax.broadcasted_iota(jnp.int32, (tk, tk), 0)
    s = lax.broadcasted_iota(jnp.int32, (tk, tk), 1)
    return (j >= s).astype(BF16)


def _prompt_tile(pair, i, bias_ref, q_ref, kt_ref, vt_ref, suffix_ref, out_ref,
                 logw_scr, blocksum_scr, total_scr, acc_scr, *, tq, tk, between=()):
    pending = list(between)

    def run_next(keep=1):
        if len(pending) > keep:
            pending.pop(0)()
    n_rep = tk // V7X_LANES
    heads = range(HEADS_PER_LANE_TILE)
    lane = lax.broadcasted_iota(jnp.int32, (tq, V7X_LANES), 1)
    q_pair = q_ref[...].astype(F32)
    q_heads = [jnp.where(lane // SB_HEAD_DIM == e, q_pair, 0.0).astype(BF16) for e in heads]
    biases = [bias_ref[pair * HEADS_PER_LANE_TILE + e] for e in heads]
    suffix = suffix_ref[...]

    def score_rows(j, slot, r0, r1, diagonal):
        kt = kt_ref[0, j]
        if diagonal:
            row = lax.broadcasted_iota(jnp.int32, (r1 - r0, tk), 0)
            col = lax.broadcasted_iota(jnp.int32, (r1 - r0, tk), 1)
            visible = col < row
        for e in heads:
            z = _dot(q_heads[e][r0:r1], kt) + biases[e]
            sp = _softplus(z)
            if diagonal:
                sp = jnp.where(visible, sp, 0.0)
            suffix_sums = _dot(sp.astype(BF16), suffix)
            log_w = z - suffix_sums
            if diagonal:
                log_w = jnp.where(visible, log_w, NEG_INF)
            logw_scr[slot, e, r0:r1, :] = log_w
            blocksum_scr[slot, e, r0:r1, :] = jnp.broadcast_to(suffix_sums[:, 0:1], (r1 - r0, V7X_LANES))

    def weigh_rows(j, slot, r0):
        vt = vt_ref[0, j]
        for e in heads:
            total = total_scr[e, r0:, :]
            a = jnp.exp(logw_scr[slot, e, r0:, :] - jnp.concatenate([total] * n_rep, axis=1))
            acc_scr[e, r0:, :] += _dot_nt(a.astype(BF16), vt)
            total_scr[e, r0:, :] = total + blocksum_scr[slot, e, r0:, :]

    total_scr[...] = jnp.zeros_like(total_scr)
    acc_scr[...] = jnp.zeros_like(acc_scr)
    ratio = tq // tk
    base = i * ratio
    n_slots = logw_scr.shape[0]
    assert ratio % n_slots == 0

    def score_diagonal(d):
        score_rows(base + d, d % n_slots, d * tk, (d + 1) * tk, True)
        if (d + 1) * tk < tq:
            score_rows(base + d, d % n_slots, (d + 1) * tk, tq, False)

    run_next()
    score_diagonal(ratio - 1)
    run_next()
    for d in range(ratio - 1, 0, -1):
        weigh_rows(base + d, d % n_slots, d * tk)
        score_diagonal(d - 1)
        run_next()
    while len(pending) > 1:
        run_next()

    def body(it, carry):
        j = base - it * n_slots
        for u in range(n_slots):
            weigh_rows(j - u, (-u) % n_slots, 0)
            score_rows(j - u - 1, (-u - 1) % n_slots, 0, tq, False)
        return carry

    lax.fori_loop(0, base // n_slots, body, 0)
    weigh_rows(0, 0, 0)
    out = acc_scr[0]
    for e in heads[1:]:
        out = jnp.where(lane // SB_HEAD_DIM == e, acc_scr[e], out)
    out_ref[...] = out.astype(BF16)
    run_next(keep=0)


def _paged_pages(first, last, q_ref, bias_ref, knew_ref, vnew_ref, k_refs, v_refs, out_ref,
                 qbd_scr, total_scr, acc_scr, *, n_q, page):
    n_rows = n_q * SB_HEADS
    d = q_ref.shape[2]
    n_rep = page // V7X_LANES
    row_d = lax.broadcasted_iota(jnp.int32, (SB_HEADS, d), 0)
    col_d = lax.broadcasted_iota(jnp.int32, (SB_HEADS, d), 1)
    head_cols = col_d // SB_HEAD_DIM == row_d

    j = lax.broadcasted_iota(jnp.int32, (page, page + V7X_LANES), 0)
    s = lax.broadcasted_iota(jnp.int32, (page, page + V7X_LANES), 1)
    suffix = ((j >= s) | (s >= page)).astype(BF16)

    def visit_stages(pages, visible):
        state = {}

        def logits():
            state["z"] = [_dot(qbd_scr[...], kt_page) + bias_ref[...] for kt_page, _ in pages]

        def suffix_sums():
            softplus = [_softplus(z) for z in state["z"]]
            if visible is not None:
                softplus = [jnp.where(visible, sp, 0.0) for sp in softplus]
            state["sums"] = [_dot(sp.astype(BF16), suffix) for sp in softplus]

        def accumulate():
            log_ws = [z - s[:, :page] for z, s in zip(state["z"], state["sums"])]
            if visible is not None:
                log_ws = [jnp.where(visible, lw, NEG_INF) for lw in log_ws]
            total = total_scr[...]
            weights = []
            for log_w, s in zip(log_ws, state["sums"]):
                weights.append(jnp.exp(log_w - jnp.concatenate([total] * n_rep, axis=1)).astype(BF16))
                total = total + s[:, page:]
            acc = acc_scr[...]
            for a, (_, vt_page) in zip(weights, pages):
                acc = acc + _dot_nt(a, vt_page)
            total_scr[...] = total
            acc_scr[...] = acc

        return [logits, suffix_sums, accumulate]

    @pl.when(first)
    def _new_tokens():
        q = q_ref[0]
        blocks = [jnp.where(head_cols, jnp.broadcast_to(q[i:i + 1, :], (SB_HEADS, d)), 0.0)
                  for i in range(n_q)]
        qbd_scr[...] = jnp.concatenate(blocks, axis=0).astype(BF16)
        total_scr[...] = jnp.zeros_like(total_scr)
        acc_scr[...] = jnp.zeros_like(acc_scr)
        q_idx = lax.broadcasted_iota(jnp.int32, (n_rows, page), 0) // SB_HEADS
        key_idx = lax.broadcasted_iota(jnp.int32, (n_rows, page), 1)
        for stage in visit_stages([(knew_ref[0], vnew_ref[0])], key_idx < q_idx):
            stage()

    def emit():
        @pl.when(last)
        def _():
            acc = acc_scr[...]
            rows = []
            for i in range(n_q):
                blk = jnp.where(head_cols, acc[i * SB_HEADS:(i + 1) * SB_HEADS, :], 0.0)
                rows.append(jnp.sum(blk, axis=0, keepdims=True))
            out_ref[0] = jnp.concatenate(rows, axis=0).astype(BF16)

    cache_pages = [(k_ref[0].astype(BF16), v_ref[0].astype(BF16)) for k_ref, v_ref in zip(k_refs, v_refs)]
    per_group = max(1, len(cache_pages) // PAGE_GROUPS_PER_STEP)
    groups = [visit_stages(cache_pages[g:g + per_group], None) for g in range(0, len(cache_pages), per_group)]
    n_stages = len(groups[0])
    waves = []
    for w in range(len(groups) + n_stages - 1):
        calls = [groups[w - st][st] for st in reversed(range(n_stages)) if 0 <= w - st < len(groups)]
        waves.append(lambda calls=calls: [call() for call in calls])
    return waves + [emit]


def _sb_attention_kernel(pt_ref, bias_ref, q_ref, kt_ref, vt_ref, suffix_ref,
                         qs_ref, bias_rows_ref, knew_ref, vnew_ref, *rest,
                         tq, tk, n_q, page, n_slots, steps_per_seq):
    del pt_ref
    k_refs, v_refs = rest[:n_slots], rest[n_slots:2 * n_slots]
    out_ref, outs_ref = rest[2 * n_slots:2 * n_slots + 2]
    logw_scr, blocksum_scr, total_scr, acc_scr, qbd_scr, totals_scr, accs_scr = rest[2 * n_slots + 2:]
    pair, i = pl.program_id(1), pl.program_id(2)
    step = (pl.program_id(0) * pl.num_programs(1) + pair) * pl.num_programs(2) + i
    group = lax.rem(step, steps_per_seq)
    sample_stages = _paged_pages(group == 0, group == steps_per_seq - 1, qs_ref, bias_rows_ref, knew_ref,
                                 vnew_ref, k_refs, v_refs, outs_ref, qbd_scr, totals_scr, accs_scr,
                                 n_q=n_q, page=page)
    _prompt_tile(pair, i, bias_ref, q_ref, kt_ref, vt_ref, suffix_ref, out_ref,
                 logw_scr, blocksum_scr, total_scr, acc_scr, tq=tq, tk=tk, between=sample_stages)


def _sb_attention(q, kt_blocks, vt_blocks, bias, q_new, bias_rows, kt_new_pad, vt_new_pad,
                  cache_kt, cache_vt, page_table, *, batch):
    t_total, d = q.shape
    t = t_total // batch
    n_blocks, tk = kt_blocks.shape[1], kt_blocks.shape[3]
    tq = min(t, SB_QUERY_BLOCK)
    assert tq % tk == 0 and t % tq == 0, (t, tq, tk)
    nq = t // tq
    n_pairs = d // V7X_LANES
    n_steps = batch * n_pairs * nq

    seqs, n_q, _ = q_new.shape
    n_pages = page_table.shape[1]
    page = cache_kt.shape[2]
    n_rows = n_q * SB_HEADS
    assert (seqs * n_pages) % n_steps == 0, (seqs, n_pages, n_steps)
    n_slots = seqs * n_pages // n_steps
    assert n_pages % n_slots == 0, (n_pages, n_slots)
    steps_per_seq = n_pages // n_slots

    step_of = lambda b, p, i: (b * n_pairs + p) * nq + i
    seq_of = lambda b, p, i: lax.div(step_of(b, p, i), steps_per_seq)
    per_seq = lambda b, p, i, pt: (seq_of(b, p, i), 0, 0)

    step_pages = page_table[:, ::-1].reshape(n_steps, n_slots)

    def page_spec(slot):
        return pl.BlockSpec((1, d, page), lambda b, p, i, pt: (pt[step_of(b, p, i), slot], 0, 0))

    kv_spec = pl.BlockSpec((1, n_blocks, V7X_LANES, tk), lambda b, p, i, pt: (b, 0, p, 0),
                           pipeline_mode=pl.Buffered(1))
    q_rows = pl.BlockSpec((tq, V7X_LANES), lambda b, p, i, pt: (b * nq + i, p))
    head_tiles = lambda lanes: pltpu.VMEM((HEADS_PER_LANE_TILE, tq, lanes), F32)
    loop_slots = math.gcd(tq // tk, SB_LOOP_UNROLL)
    slot_tiles = lambda lanes: pltpu.VMEM((loop_slots, HEADS_PER_LANE_TILE, tq, lanes), F32)
    grid_spec = pltpu.PrefetchScalarGridSpec(
        num_scalar_prefetch=1,
        grid=(batch, n_pairs, nq),
        in_specs=[
            pl.BlockSpec(memory_space=pltpu.SMEM),
            q_rows,
            kv_spec,
            kv_spec,
            pl.BlockSpec((tk, tk), lambda b, p, i, pt: (0, 0)),
            pl.BlockSpec((1, n_q, d), per_seq),
            pl.BlockSpec(bias_rows.shape, lambda b, p, i, pt: (0, 0)),
            pl.BlockSpec((1, d, page), per_seq),
            pl.BlockSpec((1, d, page), per_seq),
        ] + [page_spec(s) for s in range(n_slots)] * 2,
        out_specs=[q_rows, pl.BlockSpec((1, n_q, d), per_seq)],
        scratch_shapes=[slot_tiles(tk), slot_tiles(V7X_LANES), head_tiles(V7X_LANES), head_tiles(V7X_LANES),
                        pltpu.VMEM((n_rows, d), BF16), pltpu.VMEM((n_rows, V7X_LANES), F32),
                        pltpu.VMEM((n_rows, d), F32)],
    )
    return pl.pallas_call(
        functools.partial(_sb_attention_kernel, tq=tq, tk=tk, n_q=n_q, page=page, n_slots=n_slots,
                          steps_per_seq=steps_per_seq),
        grid_spec=grid_spec,
        out_shape=[jax.ShapeDtypeStruct((t_total, d), BF16), jax.ShapeDtypeStruct((seqs, n_q, d), BF16)],
        compiler_params=_compiler_params(("arbitrary", "arbitrary", "arbitrary")),
        name="sb_attention",
    )(step_pages, bias, q, kt_blocks, vt_blocks, _suffix_matrix(tk),
      q_new, bias_rows, kt_new_pad, vt_new_pad, *([cache_kt] * n_slots), *([cache_vt] * n_slots))


def _prep_weights(norm_mix, norm_mlp, w_in_a, b_i, b_f, head_norm_a, w_out_a, norm_kv, w_kv, w_q_b,
                  w_out_b, sb_bias, w_up, w_down, norm_final):
    qk = MLSTM_HEADS * MLSTM_DK
    vd = MLSTM_HEADS * MLSTM_DV
    w_in = w_in_a[0]
    lane_pad = ((0, 0), (0, V7X_LANES - 2 * MLSTM_HEADS))
    gate_w = jnp.pad(w_in[:, 2 * qk + 2 * vd:], lane_pad)
    gate_b = jnp.pad(jnp.concatenate([b_i[0], b_f[0]])[None, :].astype(F32), lane_pad)
    row = lambda g: g.reshape(1, -1).astype(F32)
    return dict(
        g_mix0=row(norm_mix[0]), g_mix1=row(norm_mix[1]), g_mlp0=row(norm_mlp[0]), g_mlp1=row(norm_mlp[1]),
        g_kv=row(norm_kv), g_final=row(norm_final), g_head=row(head_norm_a[0]),
        wq=w_in[:, :qk].astype(BF16),
        wkt=(w_in[:, qk:2 * qk] * (MLSTM_DK ** -0.5)).T.astype(BF16),
        wv=w_in[:, 2 * qk:2 * qk + vd].astype(BF16),
        wo=w_in[:, 2 * qk + vd:2 * qk + 2 * vd].astype(BF16),
        wg=gate_w.astype(BF16), gate_b=gate_b,
        w_out_a=w_out_a[0].astype(BF16),
        w_kvt=w_kv.T.astype(BF16),
        w_q=(w_q_b[0] * (SB_HEAD_DIM ** -0.5)).astype(BF16),
        w_out_b=w_out_b[0].astype(BF16),
        sb_bias=sb_bias[0].astype(F32),
        w_up0=w_up[0].astype(BF16), w_up1=w_up[1].astype(BF16),
        w_dn0=w_down[0].astype(BF16), w_dn1=w_down[1].astype(BF16),
    )


def _pack_state(c, n, m):
    n_rep = jnp.broadcast_to(n.astype(F32)[..., None], n.shape + (MLSTM_DV,))
    c_aug = jnp.concatenate([c.astype(F32), n_rep], axis=-1)
    return c_aug, jnp.broadcast_to(m.astype(F32)[..., None], m.shape + (V7X_LANES,))


def _unpack_state(c_aug, m_rep):
    return c_aug[..., :MLSTM_DV][None], c_aug[..., MLSTM_DV][None], m_rep[..., 0][None]


def _self_decoder(x2d, w, c_aug0, m0, *, batch, valid_len):
    q, kt, v, o, gates = _inproj(x2d, w["g_mix0"], w["wq"], w["wkt"], w["wv"], w["wo"], w["wg"])
    hg, c_aug, m_rep = _mlstm(q, kt, v, o, gates, w["gate_b"], w["g_head"], c_aug0, m0,
                              batch=batch, valid_len=valid_len)
    return hg, c_aug, m_rep


def kernel(x_prompt, x_sample, state_c, state_n, state_m, cache_k, cache_v, page_table, norm_mix, norm_mlp,
           w_in_a, b_i, b_f, head_norm_a, w_out_a, norm_kv, w_kv, w_q_b, w_out_b, sb_bias, w_up, w_down,
           norm_final):
    assert w_in_a.shape[0] == 1 and w_q_b.shape[0] == 1, "one self-decoder and one cross-decoder layer"
    w = _prep_weights(norm_mix, norm_mlp, w_in_a, b_i, b_f, head_norm_a, w_out_a, norm_kv, w_kv, w_q_b,
                      w_out_b, sb_bias, w_up, w_down, norm_final)
    bp, seq, d = x_prompt.shape
    db, dec_seq, _ = x_sample.shape
    page = cache_k.shape[1]
    kv_shape = (SB_HEADS, SB_HEAD_DIM)

    xp = x_prompt.reshape(bp * seq, d)
    zeros_c = jnp.zeros((bp, MLSTM_HEADS, MLSTM_DK, 2 * MLSTM_DV), F32)
    zeros_m = jnp.zeros((bp, MLSTM_HEADS, V7X_LANES), F32)
    hg, c_aug_p, m_rep_p = _self_decoder(xp, w, zeros_c, zeros_m, batch=bp, valid_len=MLSTM_CHUNK)
    h = _layer_tail(hg, w["w_out_a"], xp, w["g_mlp0"], w["w_up0"], w["w_dn0"], w["g_final"], final_norm=False)
    kt_p, vt_p, ktb, vtb, qb = _kvq(h, w["g_kv"], w["g_mix1"], w["w_kvt"], w["w_q"],
                                    batch=bp, key_block=min(seq, SB_KEY_BLOCK))

    n_new = db * dec_seq
    xs = x_sample.reshape(n_new, d)
    xs_pad = jnp.pad(x_sample, ((0, 0), (0, MLSTM_CHUNK - dec_seq), (0, 0))).reshape(db * MLSTM_CHUNK, d)
    c_aug0, m0 = _pack_state(state_c[0], state_n[0], state_m[0])
    hg_pad, c_aug_s, m_rep_s = _self_decoder(xs_pad, w, c_aug0, m0, batch=db, valid_len=dec_seq)
    hg_s = hg_pad.reshape(db, MLSTM_CHUNK, -1)[:, :dec_seq].reshape(n_new, -1)
    hs = _layer_tail(hg_s, w["w_out_a"], xs, w["g_mlp0"], w["w_up0"], w["w_dn0"], w["g_final"], final_norm=False)
    kt_s, vt_s, ktb_s, vtb_s, qb_s = _kvq(hs, w["g_kv"], w["g_mix1"], w["w_kvt"], w["w_q"],
                                          batch=1, key_block=n_new)

    new_page = lambda a: jnp.pad(a.reshape(d, db, dec_seq).transpose(1, 0, 2),
                                 ((0, 0), (0, 0), (0, page - dec_seq)))
    feature_major = lambda c: c.transpose(0, 2, 3, 1).reshape(c.shape[0], d, page)
    bias_rows = jnp.broadcast_to(jnp.tile(w["sb_bias"], dec_seq)[:, None], (dec_seq * SB_HEADS, page))
    attn, attn_s = _sb_attention(qb, ktb, vtb, w["sb_bias"],
                                 qb_s.astype(F32).reshape(db, dec_seq, d), bias_rows, new_page(ktb_s),
                                 new_page(vtb_s), feature_major(cache_k), feature_major(cache_v), page_table,
                                 batch=bp)

    y_p = _layer_tail(attn, w["w_out_b"], h, w["g_mlp1"], w["w_up1"], w["w_dn1"], w["g_final"], final_norm=True)
    y_s = _layer_tail(attn_s.reshape(n_new, d), w["w_out_b"], hs, w["g_mlp1"], w["w_up1"], w["w_dn1"],
                      w["g_final"], final_norm=True)
    c_p, n_p, m_p = _unpack_state(c_aug_p, m_rep_p)
    c_s, n_s, m_s = _unpack_state(c_aug_s, m_rep_s)
    token_major = lambda a: a.reshape(a.shape[0], *kv_shape, a.shape[2]).transpose(0, 3, 1, 2)
    new_rows = lambda a: a.reshape(*kv_shape, db, dec_seq).transpose(2, 3, 0, 1)

    return (y_p.reshape(bp, seq, d), y_s.reshape(db, dec_seq, d),
            c_p, n_p, m_p, token_major(kt_p), token_major(vt_p),
            c_s, n_s, m_s, new_rows(kt_s), new_rows(vt_s))
```

```python
import functools
import math

import jax
import jax.numpy as jnp
from jax import lax
from jax.experimental import pallas as pl
from jax.experimental.pallas import tpu as pltpu

F32 = jnp.float32
BF16 = jnp.bfloat16

EPS = 1e-6
LOG2E = 1.4426950408889634
MLSTM_HEADS = 8
MLSTM_DK = 64
MLSTM_DV = 128
MLSTM_CHUNK = 128
SB_HEADS = 16
SB_HEAD_DIM = 64

V7X_LANES = 128
V7X_VMEM_BYTES = 64 * 1024 * 1024
VMEM_LIMIT_BYTES = V7X_VMEM_BYTES - 8 * 1024 * 1024

HEADS_PER_LANE_TILE = V7X_LANES // SB_HEAD_DIM
NEG_INF = float("-inf")

PROJ_ROW_TILE = 512
TAIL_ROW_TILE = 1024
TAIL_FF_TILE = 1024
SB_KEY_BLOCK = 256
SB_QUERY_BLOCK = 1024
SB_LOOP_UNROLL = 4
MLSTM_SEQS_PER_STEP = 2
PAGE_GROUPS_PER_STEP = 4


def _compiler_params(semantics):
    return pltpu.CompilerParams(dimension_semantics=semantics, vmem_limit_bytes=VMEM_LIMIT_BYTES)


def _dot(a, b):
    return jnp.dot(a, b, preferred_element_type=F32)


def _dot_nt(a, b):
    return lax.dot_general(a, b, (((1,), (1,)), ((), ())), preferred_element_type=F32)


def _dot_exact(a, b):
    return jnp.dot(a, b, preferred_element_type=F32, precision=lax.Precision.HIGHEST)


def _rms_scale(x):
    return lax.rsqrt(jnp.mean(x * x, axis=-1, keepdims=True) + EPS)


def _softplus(z):
    return jnp.maximum(z, 0.0) + jnp.log(1.0 + jnp.exp2(jnp.abs(z) * (-LOG2E)))


def _row_tile(t, target):
    tile = min(t, target)
    assert t % tile == 0, (t, tile)
    return tile


def _inproj_kernel(x_ref, g_ref, wq_ref, wkt_ref, wv_ref, wo_ref, wg_ref,
                   q_ref, kt_ref, v_ref, o_ref, gate_ref):
    x = x_ref[...]
    xb = (x * _rms_scale(x) * g_ref[...]).astype(BF16)
    q_ref[...] = _dot(xb, wq_ref[...]).astype(BF16)
    kt_ref[...] = _dot_nt(wkt_ref[...], xb).astype(BF16)
    v_ref[...] = _dot(xb, wv_ref[...]).astype(BF16)
    o_ref[...] = _dot(xb, wo_ref[...])
    gate_ref[...] = _dot(xb, wg_ref[...])


def _inproj(x, gain, wq, wkt, wv, wo, wg):
    t, d = x.shape
    tm = _row_tile(t, PROJ_ROW_TILE)
    qk, vd, gw = wq.shape[1], wv.shape[1], wg.shape[1]
    full = lambda i: (0, 0)
    rows = lambda i: (i, 0)
    return pl.pallas_call(
        _inproj_kernel,
        grid=(t // tm,),
        in_specs=[
            pl.BlockSpec((tm, d), rows),
            pl.BlockSpec((1, d), full),
            pl.BlockSpec((d, qk), full),
            pl.BlockSpec((qk, d), full),
            pl.BlockSpec((d, vd), full),
            pl.BlockSpec((d, vd), full),
            pl.BlockSpec((d, gw), full),
        ],
        out_specs=[
            pl.BlockSpec((tm, qk), rows),
            pl.BlockSpec((qk, tm), lambda i: (0, i)),
            pl.BlockSpec((tm, vd), rows),
            pl.BlockSpec((tm, vd), rows),
            pl.BlockSpec((tm, gw), rows),
        ],
        out_shape=[
            jax.ShapeDtypeStruct((t, qk), BF16),
            jax.ShapeDtypeStruct((qk, t), BF16),
            jax.ShapeDtypeStruct((t, vd), BF16),
            jax.ShapeDtypeStruct((t, vd), F32),
            jax.ShapeDtypeStruct((t, gw), F32),
        ],
        compiler_params=_compiler_params(("parallel",)),
        name="mlstm_inproj",
    )(x, gain, wq, wkt, wv, wo, wg)


def _mlstm_kernel(q_ref, *rest, valid_len, n_seq):
    kt_refs = rest[:n_seq]
    v_ref, o_ref, g_ref, bias_ref, ghead_ref, c0_ref, m0_ref, hg_ref, c_ref, m_ref, rows_scr = rest[n_seq:]
    n_heads, dk, dv, chunk = MLSTM_HEADS, MLSTM_DK, MLSTM_DV, MLSTM_CHUNK
    assert chunk == V7X_LANES

    @pl.when(pl.program_id(1) == 0)
    def _load_state():
        c_ref[...] = c0_ref[...]
        m_ref[...] = m0_ref[...]

    lane = lax.broadcasted_iota(jnp.int32, (chunk, V7X_LANES), 1)
    sub = lax.broadcasted_iota(jnp.int32, (chunk, V7X_LANES), 0)
    is_f_lane = (lane >= n_heads) & (lane < 2 * n_heads)
    is_f_row = (sub >= n_heads) & (sub < 2 * n_heads)
    lower = (lane <= sub).astype(F32)
    upper = (sub <= lane).astype(F32)
    causal = lane <= sub
    ones_block = jnp.ones((chunk, dv), BF16)
    tile = lambda col: jnp.broadcast_to(col, (chunk, V7X_LANES))

    def gate_sums(s):
        gates = g_ref[s] + bias_ref[...]
        log_f = jnp.minimum(gates, 0.0) - jnp.log1p(jnp.exp(-jnp.abs(gates)))
        if valid_len < chunk:
            valid = sub < valid_len
            log_f = jnp.where(valid, log_f, 0.0)
            gates = jnp.where(valid, gates, NEG_INF)
        col_form = jnp.where(is_f_lane, log_f, jnp.where(lane < n_heads, gates, 0.0))
        row_form = col_form.T
        b_cols = _dot_exact(lower, jnp.where(is_f_lane, col_form, 0.0))
        b_rows = _dot_exact(jnp.where(is_f_row, row_form, 0.0), upper)
        rows_scr[s, 0:chunk, :] = row_form
        rows_scr[s, chunk:2 * chunk, :] = b_rows
        return b_cols

    def head_step(s, h, b_cols):
        pair, half = divmod(h, HEADS_PER_LANE_TILE)
        b_col = tile(jnp.sum(jnp.where(lane == n_heads + h, b_cols, 0.0), axis=1, keepdims=True))
        ig_row = rows_scr[s, h:h + 1, :]
        b_row = rows_scr[s, chunk + n_heads + h:chunk + n_heads + h + 1, :]
        m_prev = m_ref[s, h:h + 1, :]

        log_d = jnp.where(causal, b_col - b_row + ig_row, NEG_INF)
        m_inter = b_col + m_prev
        m_t = jnp.maximum(m_inter, tile(jnp.max(log_d, axis=1, keepdims=True)))
        w_inter = jnp.exp(m_inter - m_t)
        decay_mat = jnp.exp(log_d - m_t)

        q_pair = q_ref[s, :, pair * V7X_LANES:(pair + 1) * V7X_LANES].astype(F32)
        q_h = jnp.where(lane // dk == half, q_pair, 0.0).astype(BF16)
        kt_pair = kt_refs[s][pair * V7X_LANES:(pair + 1) * V7X_LANES, :]
        scores = _dot(q_h, kt_pair) * decay_mat

        v_aug = jnp.concatenate([v_ref[s, :, h * dv:(h + 1) * dv], ones_block], axis=1)
        c_h = c_ref[s, h]
        c_pair = jnp.concatenate([c_h, c_h], axis=0).astype(BF16)
        q_c = _dot(q_h, c_pair)
        s_v = _dot(scores.astype(BF16), v_aug)
        num = w_inter * q_c[:, :dv] + s_v[:, :dv]
        den = w_inter * q_c[:, dv:] + s_v[:, dv:]
        h_val = num / jnp.maximum(jnp.abs(den), jnp.exp(-m_t))

        mean_sq = tile(jnp.mean(h_val * h_val, axis=1, keepdims=True))
        y = h_val * lax.rsqrt(mean_sq + EPS) * ghead_ref[:, h * dv:(h + 1) * dv]
        gate_o = jax.nn.sigmoid(o_ref[s, :, h * dv:(h + 1) * dv])
        hg_ref[s, :, h * dv:(h + 1) * dv] = (gate_o * y).astype(BF16)

        m_new = m_t[chunk - 1:chunk, :]
        b_last = b_col[chunk - 1:chunk, :]
        w_k = jnp.exp(b_last - b_row + ig_row - m_new)
        decay = jnp.exp(b_last + m_prev - m_new)
        kt_h = kt_pair[half * dk:(half + 1) * dk, :].astype(F32)
        c_ref[s, h] = (jnp.concatenate([decay] * (2 * dv // V7X_LANES), axis=1) * c_h
                       + _dot((kt_h * w_k).astype(BF16), v_aug))
        m_ref[s, h:h + 1, :] = m_new

    b_cols = [gate_sums(s) for s in range(n_seq)]
    for h in range(n_heads):
        for s in range(n_seq):
            head_step(s, h, b_cols[s])


def _mlstm(q, kt, v, o, gates, gate_bias, ghead, c_aug0, m0, *, batch, valid_len):
    t_total = q.shape[0]
    chunk = MLSTM_CHUNK
    n_seq = MLSTM_SEQS_PER_STEP
    assert batch % n_seq == 0, (batch, n_seq)
    n_chunks = t_total // (batch * chunk)
    qk, vd, gw = q.shape[1], v.shape[1], gates.shape[1]
    per_seq = lambda a: a.reshape(batch, n_chunks * chunk, a.shape[1])
    rows = lambda b, c: (b, c, 0)
    full = lambda b, c: (0, 0)
    state4 = lambda b, c: (b, 0, 0, 0)
    state3 = lambda b, c: (b, 0, 0)
    kt_spec = lambda s: pl.BlockSpec((qk, chunk), lambda b, c: (0, (b * n_seq + s) * n_chunks + c))
    hg, c_aug, m_rep = pl.pallas_call(
        functools.partial(_mlstm_kernel, valid_len=valid_len, n_seq=n_seq),
        grid=(batch // n_seq, n_chunks),
        in_specs=[pl.BlockSpec((n_seq, chunk, qk), rows)] + [kt_spec(s) for s in range(n_seq)] + [
            pl.BlockSpec((n_seq, chunk, vd), rows),
            pl.BlockSpec((n_seq, chunk, vd), rows),
            pl.BlockSpec((n_seq, chunk, gw), rows),
            pl.BlockSpec((1, gw), full),
            pl.BlockSpec((1, vd), full),
            pl.BlockSpec((n_seq,) + c_aug0.shape[1:], state4),
            pl.BlockSpec((n_seq,) + m0.shape[1:], state3),
        ],
        out_specs=[
            pl.BlockSpec((n_seq, chunk, vd), rows),
            pl.BlockSpec((n_seq,) + c_aug0.shape[1:], state4),
            pl.BlockSpec((n_seq,) + m0.shape[1:], state3),
        ],
        out_shape=[
            jax.ShapeDtypeStruct((batch, n_chunks * chunk, vd), BF16),
            jax.ShapeDtypeStruct(c_aug0.shape, F32),
            jax.ShapeDtypeStruct(m0.shape, F32),
        ],
        scratch_shapes=[pltpu.VMEM((n_seq, 2 * chunk, V7X_LANES), F32)],
        compiler_params=_compiler_params(("parallel", "arbitrary")),
        name="mlstm_chunk",
    )(per_seq(q), *([kt] * n_seq), per_seq(v), per_seq(o), per_seq(gates), gate_bias, ghead, c_aug0, m0)
    return hg.reshape(t_total, vd), c_aug, m_rep


def _layer_tail_kernel(mix_ref, wout_ref, res_ref, g_ref, wup_ref, wdn_ref, gfin_ref, out_ref,
                       hn_scr, acc_scr, *, final_norm):
    j = pl.program_id(1)

    @pl.when(j == 0)
    def _start():
        h = res_ref[...] + _dot(mix_ref[...], wout_ref[...])
        hn_scr[...] = (h * _rms_scale(h) * g_ref[...]).astype(BF16)
        acc_scr[...] = h

    u = _dot(hn_scr[...], wup_ref[...])
    act = jnp.square(jnp.maximum(u, 0.0)).astype(BF16)
    acc_scr[...] += _dot(act, wdn_ref[...])

    @pl.when(j == pl.num_programs(1) - 1)
    def _finish():
        y = acc_scr[...]
        if final_norm:
            y = y * _rms_scale(y) * gfin_ref[...]
        out_ref[...] = y


def _layer_tail(mix, w_out, res, gain, w_up, w_down, gain_final, *, final_norm):
    t, d = res.shape
    k = mix.shape[1]
    ff = w_up.shape[1]
    tm = _row_tile(t, TAIL_ROW_TILE)
    tf = _row_tile(ff, TAIL_FF_TILE)
    rows = lambda i, j: (i, 0)
    full = lambda i, j: (0, 0)
    return pl.pallas_call(
        functools.partial(_layer_tail_kernel, final_norm=final_norm),
        grid=(t // tm, ff // tf),
        in_specs=[
            pl.BlockSpec((tm, k), rows),
            pl.BlockSpec((k, d), full),
            pl.BlockSpec((tm, d), rows),
            pl.BlockSpec((1, d), full),
            pl.BlockSpec((d, tf), lambda i, j: (0, j)),
            pl.BlockSpec((tf, d), lambda i, j: (j, 0)),
            pl.BlockSpec((1, d), full),
        ],
        out_specs=pl.BlockSpec((tm, d), rows),
        out_shape=jax.ShapeDtypeStruct((t, d), F32),
        scratch_shapes=[pltpu.VMEM((tm, d), BF16), pltpu.VMEM((tm, d), F32)],
        compiler_params=_compiler_params(("parallel", "arbitrary")),
        name="layer_tail",
    )(mix, w_out, res, gain, w_up, w_down, gain_final)


def _kvq_kernel(h_ref, gkv_ref, gq_ref, wkvt_ref, wq_ref, kt_ref, vt_ref, ktb_ref, vtb_ref, qb_ref):
    h = h_ref[...]
    hn = h * _rms_scale(h)
    kvt = _dot_nt(wkvt_ref[...], (hn * gkv_ref[...]).astype(BF16))
    d = kt_ref.shape[1]
    kt, vt = kvt[:d], kvt[d:]
    kt_ref[0] = kt
    vt_ref[0] = vt
    key_block = ktb_ref.shape[3]
    for blk in range(ktb_ref.shape[1]):
        cols = slice(blk * key_block, (blk + 1) * key_block)
        ktb_ref[0, blk] = kt[:, cols].astype(BF16)
        vtb_ref[0, blk] = vt[:, cols].astype(BF16)
    qb_ref[...] = _dot((hn * gq_ref[...]).astype(BF16), wq_ref[...]).astype(BF16)


def _kvq(h, gain_kv, gain_q, w_kvt, w_q, *, batch, key_block):
    t_total, d = h.shape
    t = t_total // batch
    tm = _row_tile(t, PROJ_ROW_TILE)
    assert tm % key_block == 0, (tm, key_block)
    nt = t // tm
    n = w_q.shape[1]
    rows = lambda b, i: (b * nt + i, 0)
    full = lambda b, i: (0, 0)
    feat = pl.BlockSpec((1, n, tm), lambda b, i: (b, 0, i))
    blocks = pl.BlockSpec((1, tm // key_block, n, key_block), lambda b, i: (b, i, 0, 0))
    return pl.pallas_call(
        _kvq_kernel,
        grid=(batch, nt),
        in_specs=[pl.BlockSpec((tm, d), rows), pl.BlockSpec((1, d), full), pl.BlockSpec((1, d), full),
                  pl.BlockSpec((2 * n, d), full), pl.BlockSpec((d, n), full)],
        out_specs=[feat, feat, blocks, blocks, pl.BlockSpec((tm, n), rows)],
        out_shape=[jax.ShapeDtypeStruct((batch, n, t), F32), jax.ShapeDtypeStruct((batch, n, t), F32),
                   jax.ShapeDtypeStruct((batch, t // key_block, n, key_block), BF16),
                   jax.ShapeDtypeStruct((batch, t // key_block, n, key_block), BF16),
                   jax.ShapeDtypeStruct((t_total, n), BF16)],
        compiler_params=_compiler_params(("parallel", "parallel")),
        name="kvq_proj",
    )(h, gain_kv, gain_q, w_kvt, w_q)


def _suffix_matrix(tk):
    j = lax.broadcasted_iota(jnp.int32, (tk, tk), 0)
    s = lax.broadcasted_iota(jnp.int32, (tk, tk), 1)
    return (j >= s).astype(BF16)


def _prompt_tile(pair, i, bias_ref, q_ref, kt_ref, vt_ref, suffix_ref, out_ref,
                 logw_scr, blocksum_scr, total_scr, acc_scr, *, tq, tk, between=()):
    pending = list(between)

    def run_next(keep=1):
        if len(pending) > keep:
            pending.pop(0)()
    n_rep = tk // V7X_LANES
    heads = range(HEADS_PER_LANE_TILE)
    lane = lax.broadcasted_iota(jnp.int32, (tq, V7X_LANES), 1)
    q_pair = q_ref[...].astype(F32)
    q_heads = [jnp.where(lane // SB_HEAD_DIM == e, q_pair, 0.0).astype(BF16) for e in heads]
    biases = [bias_ref[pair * HEADS_PER_LANE_TILE + e] for e in heads]
    suffix = suffix_ref[...]

    def score_rows(j, slot, r0, r1, diagonal):
        kt = kt_ref[0, j]
        if diagonal:
            row = lax.broadcasted_iota(jnp.int32, (r1 - r0, tk), 0)
            col = lax.broadcasted_iota(jnp.int32, (r1 - r0, tk), 1)
            visible = col < row
        for e in heads:
            z = _dot(q_heads[e][r0:r1], kt) + biases[e]
            sp = _softplus(z)
            if diagonal:
                sp = jnp.where(visible, sp, 0.0)
            suffix_sums = _dot(sp.astype(BF16), suffix)
            log_w = z - suffix_sums
            if diagonal:
                log_w = jnp.where(visible, log_w, NEG_INF)
            logw_scr[slot, e, r0:r1, :] = log_w
            blocksum_scr[slot, e, r0:r1, :] = jnp.broadcast_to(suffix_sums[:, 0:1], (r1 - r0, V7X_LANES))

    def weigh_rows(j, slot, r0):
        vt = vt_ref[0, j]
        for e in heads:
            total = total_scr[e, r0:, :]
            a = jnp.exp(logw_scr[slot, e, r0:, :] - jnp.concatenate([total] * n_rep, axis=1))
            acc_scr[e, r0:, :] += _dot_nt(a.astype(BF16), vt)
            total_scr[e, r0:, :] = total + blocksum_scr[slot, e, r0:, :]

    total_scr[...] = jnp.zeros_like(total_scr)
    acc_scr[...] = jnp.zeros_like(acc_scr)
    ratio = tq // tk
    base = i * ratio
    n_slots = logw_scr.shape[0]
    assert ratio % n_slots == 0

    def score_diagonal(d):
        score_rows(base + d, d % n_slots, d * tk, (d + 1) * tk, True)
        if (d + 1) * tk < tq:
            score_rows(base + d, d % n_slots, (d + 1) * tk, tq, False)

    run_next()
    score_diagonal(ratio - 1)
    run_next()
    for d in range(ratio - 1, 0, -1):
        weigh_rows(base + d, d % n_slots, d * tk)
        score_diagonal(d - 1)
        run_next()
    while len(pending) > 1:
        run_next()

    def body(it, carry):
        j = base - it * n_slots
        for u in range(n_slots):
            weigh_rows(j - u, (-u) % n_slots, 0)
            score_rows(j - u - 1, (-u - 1) % n_slots, 0, tq, False)
        return carry

    lax.fori_loop(0, base // n_slots, body, 0)
    weigh_rows(0, 0, 0)
    out = acc_scr[0]
    for e in heads[1:]:
        out = jnp.where(lane // SB_HEAD_DIM == e, acc_scr[e], out)
    out_ref[...] = out.astype(BF16)
    run_next(keep=0)


def _paged_pages(first, last, q_ref, bias_ref, knew_ref, vnew_ref, k_refs, v_refs, out_ref,
                 qbd_scr, total_scr, acc_scr, *, n_q, page):
    n_rows = n_q * SB_HEADS
    d = q_ref.shape[2]
    n_rep = page // V7X_LANES
    row_d = lax.broadcasted_iota(jnp.int32, (SB_HEADS, d), 0)
    col_d = lax.broadcasted_iota(jnp.int32, (SB_HEADS, d), 1)
    head_cols = col_d // SB_HEAD_DIM == row_d

    j = lax.broadcasted_iota(jnp.int32, (page, page + V7X_LANES), 0)
    s = lax.broadcasted_iota(jnp.int32, (page, page + V7X_LANES), 1)
    suffix = ((j >= s) | (s >= page)).astype(BF16)

    def visit_stages(pages, visible):
        state = {}

        def logits():
            state["z"] = [_dot(qbd_scr[...], kt_page) + bias_ref[...] for kt_page, _ in pages]

        def suffix_sums():
            softplus = [_softplus(z) for z in state["z"]]
            if visible is not None:
                softplus = [jnp.where(visible, sp, 0.0) for sp in softplus]
            state["sums"] = [_dot(sp.astype(BF16), suffix) for sp in softplus]

        def accumulate():
            log_ws = [z - s[:, :page] for z, s in zip(state["z"], state["sums"])]
            if visible is not None:
                log_ws = [jnp.where(visible, lw, NEG_INF) for lw in log_ws]
            total = total_scr[...]
            weights = []
            for log_w, s in zip(log_ws, state["sums"]):
                weights.append(jnp.exp(log_w - jnp.concatenate([total] * n_rep, axis=1)).astype(BF16))
                total = total + s[:, page:]
            acc = acc_scr[...]
            for a, (_, vt_page) in zip(weights, pages):
                acc = acc + _dot_nt(a, vt_page)
            total_scr[...] = total
            acc_scr[...] = acc

        return [logits, suffix_sums, accumulate]

    @pl.when(first)
    def _new_tokens():
        q = q_ref[0]
        blocks = [jnp.where(head_cols, jnp.broadcast_to(q[i:i + 1, :], (SB_HEADS, d)), 0.0)
                  for i in range(n_q)]
        qbd_scr[...] = jnp.concatenate(blocks, axis=0).astype(BF16)
        total_scr[...] = jnp.zeros_like(total_scr)
        acc_scr[...] = jnp.zeros_like(acc_scr)
        q_idx = lax.broadcasted_iota(jnp.int32, (n_rows, page), 0) // SB_HEADS
        key_idx = lax.broadcasted_iota(jnp.int32, (n_rows, page), 1)
        for stage in visit_stages([(knew_ref[0], vnew_ref[0])], key_idx < q_idx):
            stage()

    def emit():
        @pl.when(last)
        def _():
            acc = acc_scr[...]
            rows = []
            for i in range(n_q):
                blk = jnp.where(head_cols, acc[i * SB_HEADS:(i + 1) * SB_HEADS, :], 0.0)
                rows.append(jnp.sum(blk, axis=0, keepdims=True))
            out_ref[0] = jnp.concatenate(rows, axis=0).astype(BF16)

    cache_pages = [(k_ref[...].astype(BF16), v_ref[...].astype(BF16)) for k_ref, v_ref in zip(k_refs, v_refs)]
    per_group = max(1, len(cache_pages) // PAGE_GROUPS_PER_STEP)
    groups = [visit_stages(cache_pages[g:g + per_group], None) for g in range(0, len(cache_pages), per_group)]
    n_stages = len(groups[0])
    waves = []
    for w in range(len(groups) + n_stages - 1):
        calls = [groups[w - st][st] for st in reversed(range(n_stages)) if 0 <= w - st < len(groups)]
        waves.append(lambda calls=calls: [call() for call in calls])
    return waves + [emit]


def _sb_attention_kernel(pages_ref, bias_ref, q_ref, kt_ref, vt_ref, suffix_ref,
                         qs_ref, bias_rows_ref, knew_ref, vnew_ref, cache_k_hbm, cache_v_hbm,
                         out_ref, outs_ref,
                         logw_scr, blocksum_scr, total_scr, acc_scr, qbd_scr, totals_scr, accs_scr,
                         kbuf, vbuf, page_sems, *, tq, tk, n_q, page, n_slots, steps_per_seq):
    pair, i = pl.program_id(1), pl.program_id(2)
    n_steps = pl.num_programs(0) * pl.num_programs(1) * pl.num_programs(2)
    step = (pl.program_id(0) * pl.num_programs(1) + pair) * pl.num_programs(2) + i
    group = lax.rem(step, steps_per_seq)

    def page_copies(for_step, half):
        copies = []
        for slot in range(n_slots):
            phys = pages_ref[for_step, slot]
            copies.append(pltpu.make_async_copy(cache_k_hbm.at[phys], kbuf.at[half, slot], page_sems.at[half, 0]))
            copies.append(pltpu.make_async_copy(cache_v_hbm.at[phys], vbuf.at[half, slot], page_sems.at[half, 1]))
        return copies

    half = lax.rem(step, 2)

    @pl.when(step == 0)
    def _first_fetch():
        for copy in page_copies(0, 0):
            copy.start()

    @pl.when(step + 1 < n_steps)
    def _prefetch():
        for copy in page_copies(step + 1, 1 - half):
            copy.start()

    for copy in page_copies(step, half):
        copy.wait()
    k_refs = [kbuf.at[half, slot] for slot in range(n_slots)]
    v_refs = [vbuf.at[half, slot] for slot in range(n_slots)]
    sample_stages = _paged_pages(group == 0, group == steps_per_seq - 1, qs_ref, bias_rows_ref, knew_ref,
                                 vnew_ref, k_refs, v_refs, outs_ref, qbd_scr, totals_scr, accs_scr,
                                 n_q=n_q, page=page)
    _prompt_tile(pair, i, bias_ref, q_ref, kt_ref, vt_ref, suffix_ref, out_ref,
                 logw_scr, blocksum_scr, total_scr, acc_scr, tq=tq, tk=tk, between=sample_stages)


def _sb_attention(q, kt_blocks, vt_blocks, bias, q_new, bias_rows, kt_new_pad, vt_new_pad,
                  cache_kt, cache_vt, page_table, *, batch):
    t_total, d = q.shape
    t = t_total // batch
    n_blocks, tk = kt_blocks.shape[1], kt_blocks.shape[3]
    tq = min(t, SB_QUERY_BLOCK)
    assert tq % tk == 0 and t % tq == 0, (t, tq, tk)
    nq = t // tq
    n_pairs = d // V7X_LANES
    n_steps = batch * n_pairs * nq

    seqs, n_q, _ = q_new.shape
    n_pages = page_table.shape[1]
    page = cache_kt.shape[2]
    n_rows = n_q * SB_HEADS
    assert (seqs * n_pages) % n_steps == 0, (seqs, n_pages, n_steps)
    n_slots = seqs * n_pages // n_steps
    assert n_pages % n_slots == 0, (n_pages, n_slots)
    steps_per_seq = n_pages // n_slots

    step_of = lambda b, p, i: (b * n_pairs + p) * nq + i
    seq_of = lambda b, p, i: lax.div(step_of(b, p, i), steps_per_seq)
    per_seq = lambda b, p, i, pt: (seq_of(b, p, i), 0, 0)

    step_pages = page_table[:, ::-1].reshape(n_steps, n_slots)

    page_ring = pltpu.VMEM((2, n_slots, d, page), cache_kt.dtype)

    kv_spec = pl.BlockSpec((1, n_blocks, V7X_LANES, tk), lambda b, p, i, pt: (b, 0, p, 0),
                           pipeline_mode=pl.Buffered(1))
    q_rows = pl.BlockSpec((tq, V7X_LANES), lambda b, p, i, pt: (b * nq + i, p))
    head_tiles = lambda lanes: pltpu.VMEM((HEADS_PER_LANE_TILE, tq, lanes), F32)
    loop_slots = math.gcd(tq // tk, SB_LOOP_UNROLL)
    slot_tiles = lambda lanes: pltpu.VMEM((loop_slots, HEADS_PER_LANE_TILE, tq, lanes), F32)
    grid_spec = pltpu.PrefetchScalarGridSpec(
        num_scalar_prefetch=1,
        grid=(batch, n_pairs, nq),
        in_specs=[
            pl.BlockSpec(memory_space=pltpu.SMEM),
            q_rows,
            kv_spec,
            kv_spec,
            pl.BlockSpec((tk, tk), lambda b, p, i, pt: (0, 0)),
            pl.BlockSpec((1, n_q, d), per_seq),
            pl.BlockSpec(bias_rows.shape, lambda b, p, i, pt: (0, 0)),
            pl.BlockSpec((1, d, page), per_seq),
            pl.BlockSpec((1, d, page), per_seq),
            pl.BlockSpec(memory_space=pl.ANY),
            pl.BlockSpec(memory_space=pl.ANY),
        ],
        out_specs=[q_rows, pl.BlockSpec((1, n_q, d), per_seq)],
        scratch_shapes=[slot_tiles(tk), slot_tiles(V7X_LANES), head_tiles(V7X_LANES), head_tiles(V7X_LANES),
                        pltpu.VMEM((n_rows, d), BF16), pltpu.VMEM((n_rows, V7X_LANES), F32),
                        pltpu.VMEM((n_rows, d), F32),
                        page_ring, page_ring, pltpu.SemaphoreType.DMA((2, 2))],
    )
    return pl.pallas_call(
        functools.partial(_sb_attention_kernel, tq=tq, tk=tk, n_q=n_q, page=page, n_slots=n_slots,
                          steps_per_seq=steps_per_seq),
        grid_spec=grid_spec,
        out_shape=[jax.ShapeDtypeStruct((t_total, d), BF16), jax.ShapeDtypeStruct((seqs, n_q, d), BF16)],
        compiler_params=_compiler_params(("arbitrary", "arbitrary", "arbitrary")),
        name="sb_attention",
    )(step_pages, bias, q, kt_blocks, vt_blocks, _suffix_matrix(tk),
      q_new, bias_rows, kt_new_pad, vt_new_pad, cache_kt, cache_vt)


def _prep_weights(norm_mix, norm_mlp, w_in_a, b_i, b_f, head_norm_a, w_out_a, norm_kv, w_kv, w_q_b,
                  w_out_b, sb_bias, w_up, w_down, norm_final):
    qk = MLSTM_HEADS * MLSTM_DK
    vd = MLSTM_HEADS * MLSTM_DV
    w_in = w_in_a[0]
    lane_pad = ((0, 0), (0, V7X_LANES - 2 * MLSTM_HEADS))
    gate_w = jnp.pad(w_in[:, 2 * qk + 2 * vd:], lane_pad)
    gate_b = jnp.pad(jnp.concatenate([b_i[0], b_f[0]])[None, :].astype(F32), lane_pad)
    row = lambda g: g.reshape(1, -1).astype(F32)
    return dict(
        g_mix0=row(norm_mix[0]), g_mix1=row(norm_mix[1]), g_mlp0=row(norm_mlp[0]), g_mlp1=row(norm_mlp[1]),
        g_kv=row(norm_kv), g_final=row(norm_final), g_head=row(head_norm_a[0]),
        wq=w_in[:, :qk].astype(BF16),
        wkt=(w_in[:, qk:2 * qk] * (MLSTM_DK ** -0.5)).T.astype(BF16),
        wv=w_in[:, 2 * qk:2 * qk + vd].astype(BF16),
        wo=w_in[:, 2 * qk + vd:2 * qk + 2 * vd].astype(BF16),
        wg=gate_w.astype(BF16), gate_b=gate_b,
        w_out_a=w_out_a[0].astype(BF16),
        w_kvt=w_kv.T.astype(BF16),
        w_q=(w_q_b[0] * (SB_HEAD_DIM ** -0.5)).astype(BF16),
        w_out_b=w_out_b[0].astype(BF16),
        sb_bias=sb_bias[0].astype(F32),
        w_up0=w_up[0].astype(BF16), w_up1=w_up[1].astype(BF16),
        w_dn0=w_down[0].astype(BF16), w_dn1=w_down[1].astype(BF16),
    )


def _pack_state(c, n, m):
    n_rep = jnp.broadcast_to(n.astype(F32)[..., None], n.shape + (MLSTM_DV,))
    c_aug = jnp.concatenate([c.astype(F32), n_rep], axis=-1)
    return c_aug, jnp.broadcast_to(m.astype(F32)[..., None], m.shape + (V7X_LANES,))


def _unpack_state(c_aug, m_rep):
    return c_aug[..., :MLSTM_DV][None], c_aug[..., MLSTM_DV][None], m_rep[..., 0][None]


def _self_decoder(x2d, w, c_aug0, m0, *, batch, valid_len):
    q, kt, v, o, gates = _inproj(x2d, w["g_mix0"], w["wq"], w["wkt"], w["wv"], w["wo"], w["wg"])
    hg, c_aug, m_rep = _mlstm(q, kt, v, o, gates, w["gate_b"], w["g_head"], c_aug0, m0,
                              batch=batch, valid_len=valid_len)
    return hg, c_aug, m_rep


def kernel(x_prompt, x_sample, state_c, state_n, state_m, cache_k, cache_v, page_table, norm_mix, norm_mlp,
           w_in_a, b_i, b_f, head_norm_a, w_out_a, norm_kv, w_kv, w_q_b, w_out_b, sb_bias, w_up, w_down,
           norm_final):
    assert w_in_a.shape[0] == 1 and w_q_b.shape[0] == 1, "one self-decoder and one cross-decoder layer"
    w = _prep_weights(norm_mix, norm_mlp, w_in_a, b_i, b_f, head_norm_a, w_out_a, norm_kv, w_kv, w_q_b,
                      w_out_b, sb_bias, w_up, w_down, norm_final)
    bp, seq, d = x_prompt.shape
    db, dec_seq, _ = x_sample.shape
    page = cache_k.shape[1]
    kv_shape = (SB_HEADS, SB_HEAD_DIM)

    xp = x_prompt.reshape(bp * seq, d)
    zeros_c = jnp.zeros((bp, MLSTM_HEADS, MLSTM_DK, 2 * MLSTM_DV), F32)
    zeros_m = jnp.zeros((bp, MLSTM_HEADS, V7X_LANES), F32)
    hg, c_aug_p, m_rep_p = _self_decoder(xp, w, zeros_c, zeros_m, batch=bp, valid_len=MLSTM_CHUNK)
    h = _layer_tail(hg, w["w_out_a"], xp, w["g_mlp0"], w["w_up0"], w["w_dn0"], w["g_final"], final_norm=False)
    kt_p, vt_p, ktb, vtb, qb = _kvq(h, w["g_kv"], w["g_mix1"], w["w_kvt"], w["w_q"],
                                    batch=bp, key_block=min(seq, SB_KEY_BLOCK))

    n_new = db * dec_seq
    xs = x_sample.reshape(n_new, d)
    xs_pad = jnp.pad(x_sample, ((0, 0), (0, MLSTM_CHUNK - dec_seq), (0, 0))).reshape(db * MLSTM_CHUNK, d)
    c_aug0, m0 = _pack_state(state_c[0], state_n[0], state_m[0])
    hg_pad, c_aug_s, m_rep_s = _self_decoder(xs_pad, w, c_aug0, m0, batch=db, valid_len=dec_seq)
    hg_s = hg_pad.reshape(db, MLSTM_CHUNK, -1)[:, :dec_seq].reshape(n_new, -1)
    hs = _layer_tail(hg_s, w["w_out_a"], xs, w["g_mlp0"], w["w_up0"], w["w_dn0"], w["g_final"], final_norm=False)
    kt_s, vt_s, ktb_s, vtb_s, qb_s = _kvq(hs, w["g_kv"], w["g_mix1"], w["w_kvt"], w["w_q"],
                                          batch=1, key_block=n_new)

    new_page = lambda a: jnp.pad(a.reshape(d, db, dec_seq).transpose(1, 0, 2),
                                 ((0, 0), (0, 0), (0, page - dec_seq)))
    feature_major = lambda c: c.transpose(0, 2, 3, 1).reshape(c.shape[0], d, page)
    bias_rows = jnp.broadcast_to(jnp.tile(w["sb_bias"], dec_seq)[:, None], (dec_seq * SB_HEADS, page))
    attn, attn_s = _sb_attention(qb, ktb, vtb, w["sb_bias"],
                                 qb_s.astype(F32).reshape(db, dec_seq, d), bias_rows, new_page(ktb_s),
                                 new_page(vtb_s), feature_major(cache_k), feature_major(cache_v), page_table,
                                 batch=bp)

    y_p = _layer_tail(attn, w["w_out_b"], h, w["g_mlp1"], w["w_up1"], w["w_dn1"], w["g_final"], final_norm=True)
    y_s = _layer_tail(attn_s.reshape(n_new, d), w["w_out_b"], hs, w["g_mlp1"], w["w_up1"], w["w_dn1"],
                      w["g_final"], final_norm=True)
    c_p, n_p, m_p = _unpack_state(c_aug_p, m_rep_p)
    c_s, n_s, m_s = _unpack_state(c_aug_s, m_rep_s)
    token_major = lambda a: a.reshape(a.shape[0], *kv_shape, a.shape[2]).transpose(0, 3, 1, 2)
    new_rows = lambda a: a.reshape(*kv_shape, db, dec_seq).transpose(2, 3, 0, 1)

    return (y_p.reshape(bp, seq, d), y_s.reshape(db, dec_seq, d),
            c_p, n_p, m_p, token_major(kt_p), token_major(vt_p),
            c_s, n_s, m_s, new_rows(kt_s), new_rows(vt_s))
```
